```python
import math, functools
import jax, jax.numpy as jnp
from jax import lax
import numpy as np

D_MODEL = 1024
BATCH = 8
SEQ = 4096
DEPTH = 1
DEC_BATCH = 32
DEC_SEQ = 32
PAST_LEN = 4096

CHUNK = 64
Q_BLOCK = 128
SB_HEADS = 8
SB_HEAD_DIM = 64
SB_WIDTH = SB_HEADS * SB_HEAD_DIM
SB_SCALE = SB_HEAD_DIM ** -0.5
HG_HEADS = 4
HG_KEY_DIM = 128
HG_VAL_DIM = 128
HG_KEY_WIDTH = HG_HEADS * HG_KEY_DIM
HG_VAL_WIDTH = HG_HEADS * HG_VAL_DIM
MIX_WIDTH = SB_WIDTH + HG_VAL_WIDTH
IN_WIDTHS = (SB_WIDTH, SB_WIDTH, SB_WIDTH, HG_KEY_WIDTH, HG_KEY_WIDTH, HG_VAL_WIDTH, HG_VAL_WIDTH)
IN_WIDTH = sum(IN_WIDTHS)
IN_SPLIT_POINTS = tuple(int(v) for v in np.cumsum(IN_WIDTHS)[:-1])
D_FF = 2816
EPS = 1e-6

kernel_name = "stickbreak_hgrn2_macaron_stream_step"


def rms_norm(x, g):
    xf = x.astype(jnp.float32)
    y = xf * lax.rsqrt(jnp.mean(xf * xf, axis=-1, keepdims=True) + EPS)
    return (y * g.astype(jnp.float32)).astype(x.dtype)


def swiglu(x, w_in, w_out):
    a, b = jnp.split(x @ w_in, 2, axis=-1)
    return (jax.nn.silu(a) * b) @ w_out


def sb_attend(q, k, v, q_pos, k_pos):
    z = jnp.einsum('bqhd,bkhd->bhqk', q.astype(jnp.float32), k.astype(jnp.float32)) * SB_SCALE
    mask = k_pos[None, :] < q_pos[:, None]
    log_beta = jax.nn.log_sigmoid(z)
    log_keep = jnp.where(mask, jax.nn.log_sigmoid(-z), 0.0)
    tail = lax.cumsum(log_keep, axis=3, reverse=True) - log_keep
    w = jnp.where(mask, jnp.exp(log_beta + tail), 0.0)
    return jnp.einsum('bhqk,bkhd->bqhd', w.astype(v.dtype), v)


def sb_prompt(q, k, v):
    B, T = q.shape[0], q.shape[1]
    n_blocks = T // Q_BLOCK
    k_pos = jnp.arange(T)

    def one_block(i):
        start = i * Q_BLOCK
        qb = lax.dynamic_slice_in_dim(q, start, Q_BLOCK, axis=1)
        return sb_attend(qb, k, v, start + jnp.arange(Q_BLOCK), k_pos)

    out = lax.map(one_block, jnp.arange(n_blocks))
    return out.transpose(1, 0, 2, 3, 4).reshape(B, T, SB_HEADS, SB_HEAD_DIM)


def sb_sample(q, k, v, ck, cv):
    P, T = ck.shape[1], q.shape[1]
    k_all = jnp.concatenate([ck.astype(k.dtype), k], axis=1)
    v_all = jnp.concatenate([cv.astype(v.dtype), v], axis=1)
    return sb_attend(q, k_all, v_all, P + jnp.arange(T), jnp.arange(P + T))


def hgrn2_scan(q, k, v, log_f, S0, chunk):
    B, T, H, dk = q.shape
    dv = v.shape[-1]
    n = T // chunk

    def to_chunks(a):
        return a.reshape(B, n, chunk, H, a.shape[-1]).transpose(1, 0, 3, 2, 4)

    qc, kc, vc, gc = to_chunks(q), to_chunks(k), to_chunks(v), to_chunks(log_f)
    causal = jnp.tril(jnp.ones((chunk, chunk), dtype=bool))

    def step(S, inp):
        qi, ki, vi, gi = inp
        b = jnp.cumsum(gi, axis=2)
        o_inter = jnp.einsum('bhtc,bhcv->bhtv', qi * jnp.exp(b), S)
        diff = b[:, :, :, None, :] - b[:, :, None, :, :]
        decay = jnp.exp(jnp.where(causal[:, :, None], diff, -jnp.inf))
        att = jnp.einsum('bhtc,bhsc,bhtsc->bhts', qi, ki, decay)
        o = o_inter + jnp.einsum('bhts,bhsv->bhtv', att, vi)
        b_last = b[:, :, -1:, :]
        S_new = jnp.exp(b_last[:, :, 0, :])[..., None] * S + jnp.einsum(
            'bhsc,bhsv->bhcv', ki * jnp.exp(b_last - b), vi)
        return S_new, o

    S_fin, o = lax.scan(step, S0.astype(jnp.float32), (qc, kc, vc, gc))
    o = o.transpose(1, 0, 3, 2, 4).reshape(B, T, H, dv)
    return o, S_fin


def _layer(x, ffn1_norm, ffn1_w_in, ffn1_w_out, mix_norm, w_in, sb_q_gain, sb_k_gain,
           lb, sb_out_gain, hg_out_gain, w_out, ffn2_norm, ffn2_w_in, ffn2_w_out,
           sb_mix, S0, hg_chunk):
    B, T, _ = x.shape
    x = x + 0.5 * swiglu(rms_norm(x, ffn1_norm), ffn1_w_in, ffn1_w_out)
    h = rms_norm(x, mix_norm)
    sq, sk, sv, hq, hf, hi, hg = jnp.split(h @ w_in, IN_SPLIT_POINTS, axis=-1)

    def heads(a, n):
        return a.reshape(B, T, n, a.shape[-1] // n)

    sq = rms_norm(heads(sq, SB_HEADS), sb_q_gain)
    sk = rms_norm(heads(sk, SB_HEADS), sb_k_gain)
    sv = heads(sv, SB_HEADS)
    sb_o = rms_norm(sb_mix(sq, sk, sv), sb_out_gain).reshape(B, T, SB_WIDTH)
    hf32 = hf.astype(jnp.float32)
    f = lb + (1.0 - lb) * jax.nn.sigmoid(hf32)
    log_f = jnp.log(f)
    k_in = (1.0 - lb) * jax.nn.sigmoid(-hf32)
    q_h = jax.nn.silu(hq.astype(jnp.float32))
    o_h, S_new = hgrn2_scan(heads(q_h, HG_HEADS), heads(k_in, HG_HEADS),
                            heads(hi.astype(jnp.float32), HG_HEADS), heads(log_f, HG_HEADS),
                            S0, hg_chunk)
    o_h = rms_norm(o_h, hg_out_gain) * jax.nn.silu(heads(hg, HG_HEADS).astype(jnp.float32))
    mix = jnp.concatenate([sb_o, o_h.reshape(B, T, HG_VAL_WIDTH).astype(x.dtype)], axis=-1)
    x = x + mix @ w_out
    x = x + 0.5 * swiglu(rms_norm(x, ffn2_norm), ffn2_w_in, ffn2_w_out)
    return x, sk, sv, S_new.astype(x.dtype)


def setup_inputs(seed: int = 0) -> dict:
    key = jax.random.key(seed)
    ks = jax.random.split(key, 20)

    def nrm(k, shape, scale):
        return jax.random.normal(k, shape, jnp.float32) * scale

    def gain(k, shape):
        return 1.0 + 0.05 * jax.random.normal(k, shape, jnp.float32)

    return {
        "x_prompt": nrm(ks[0], (BATCH, SEQ, D_MODEL), 1.0),
        "x_sample": nrm(ks[1], (DEC_BATCH, DEC_SEQ, D_MODEL), 1.0),
        "cache_sb_k": nrm(ks[2], (DEPTH, DEC_BATCH, PAST_LEN, SB_HEADS, SB_HEAD_DIM), 1.0),
        "cache_sb_v": nrm(ks[3], (DEPTH, DEC_BATCH, PAST_LEN, SB_HEADS, SB_HEAD_DIM), 1.0),
        "state_hgrn": nrm(ks[4], (DEPTH, DEC_BATCH, HG_HEADS, HG_KEY_DIM, HG_VAL_DIM), 0.5),
        "ffn1_norm": gain(ks[5], (DEPTH, D_MODEL)),
        "ffn1_w_in": nrm(ks[6], (DEPTH, D_MODEL, 2 * D_FF), D_MODEL ** -0.5),
        "ffn1_w_out": nrm(ks[7], (DEPTH, D_FF, D_MODEL), D_FF ** -0.5),
        "mix_norm": gain(ks[8], (DEPTH, D_MODEL)),
        "w_in": nrm(ks[9], (DEPTH, D_MODEL, IN_WIDTH), D_MODEL ** -0.5),
        "sb_q_gain": gain(ks[10], (DEPTH, SB_HEAD_DIM)),
        "sb_k_gain": gain(ks[11], (DEPTH, SB_HEAD_DIM)),
        "hg_lb_logits": nrm(ks[12], (DEPTH + 1, HG_KEY_WIDTH), 0.5),
        "sb_out_gain": gain(ks[13], (DEPTH, SB_HEAD_DIM)),
        "hg_out_gain": gain(ks[14], (DEPTH, HG_VAL_DIM)),
        "w_out": nrm(ks[15], (DEPTH, MIX_WIDTH, D_MODEL), MIX_WIDTH ** -0.5),
        "ffn2_norm": gain(ks[16], (DEPTH, D_MODEL)),
        "ffn2_w_in": nrm(ks[17], (DEPTH, D_MODEL, 2 * D_FF), D_MODEL ** -0.5),
        "ffn2_w_out": nrm(ks[18], (DEPTH, D_FF, D_MODEL), D_FF ** -0.5),
    }


def reference(x_prompt, x_sample, cache_sb_k, cache_sb_v, state_hgrn,
              ffn1_norm, ffn1_w_in, ffn1_w_out, mix_norm, w_in, sb_q_gain, sb_k_gain,
              hg_lb_logits, sb_out_gain, hg_out_gain, w_out, ffn2_norm, ffn2_w_in, ffn2_w_out):
    lb_all = jnp.cumsum(jax.nn.softmax(hg_lb_logits.astype(jnp.float32), axis=0), axis=0)

    xp, xs = x_prompt, x_sample
    kp_list, vp_list, sp_list, ks_list, vs_list, ss_list = [], [], [], [], [], []
    for l in range(DEPTH):
        w_l = (ffn1_norm[l], ffn1_w_in[l], ffn1_w_out[l], mix_norm[l], w_in[l],
               sb_q_gain[l], sb_k_gain[l], lb_all[l], sb_out_gain[l], hg_out_gain[l],
               w_out[l], ffn2_norm[l], ffn2_w_in[l], ffn2_w_out[l])
        S0_p = jnp.zeros((xp.shape[0], HG_HEADS, HG_KEY_DIM, HG_VAL_DIM), jnp.float32)
        xp, kp, vp, sp = _layer(xp, *w_l, sb_mix=sb_prompt, S0=S0_p, hg_chunk=CHUNK)
        sb_mix_s = functools.partial(sb_sample, ck=cache_sb_k[l], cv=cache_sb_v[l])
        xs, kn, vn, sn = _layer(xs, *w_l, sb_mix=sb_mix_s, S0=state_hgrn[l],
                                hg_chunk=xs.shape[1])
        kp_list.append(kp); vp_list.append(vp); sp_list.append(sp)
        ks_list.append(kn); vs_list.append(vn); ss_list.append(sn)

    new_k_prompt = jnp.stack(kp_list, axis=0)
    new_v_prompt = jnp.stack(vp_list, axis=0)
    new_state_prompt = jnp.stack(sp_list, axis=0)
    new_k_sample = jnp.stack(ks_list, axis=0)
    new_v_sample = jnp.stack(vs_list, axis=0)
    new_state_sample = jnp.stack(ss_list, axis=0)
    return (xp, xs, new_k_prompt, new_v_prompt, new_state_prompt,
            new_k_sample, new_v_sample, new_state_sample)
```

```python
import functools

import jax
import jax.numpy as jnp
from jax import lax
from jax.experimental import pallas as pl
from jax.experimental.pallas import tpu as pltpu

EPS = 1e-6
SB_HEADS = 8
SB_HEAD_DIM = 64
SB_WIDTH = SB_HEADS * SB_HEAD_DIM
SB_SCALE = SB_HEAD_DIM ** -0.5
HG_HEADS = 4
HG_DIM = 128
HG_WIDTH = HG_HEADS * HG_DIM
LANES = 128
SUB = 8
HEAD_PAIRS = SB_WIDTH // LANES

BF16 = jnp.bfloat16
F32 = jnp.float32

VMEM_LIMIT = 56 * 1024 * 1024


def _dot(a, b):
    return jnp.dot(a, b, preferred_element_type=F32)


def _dot_nt(a, b):
    return lax.dot_general(a, b, (((1,), (1,)), ((), ())), preferred_element_type=F32)


def _dot_tn(a, b):
    return lax.dot_general(a, b, (((0,), (0,)), ((), ())), preferred_element_type=F32)


def _sigmoid(x):
    return 1.0 / (1.0 + jnp.exp(-x))


def _rms_rows(x, g):
    ms = jnp.mean(x * x, axis=-1, keepdims=True)
    return x * lax.rsqrt(ms + EPS) * g


def _params(n_grid):
    return pltpu.CompilerParams(
        dimension_semantics=("arbitrary",) * n_grid, vmem_limit_bytes=VMEM_LIMIT)


def _ffn_kernel(*refs, with_mix):
    if with_mix:
        (x_ref, ma_ref, mb_ref, woa_ref, wob_ref, g_ref, wa_ref, wb_ref, wo_ref,
         o_ref, h_ref) = refs
    else:
        x_ref, g_ref, wa_ref, wb_ref, wo_ref, o_ref, h_ref = refs

    @pl.when(pl.program_id(1) == 0)
    def _():
        x = x_ref[...]
        if with_mix:
            x = x + _dot(ma_ref[...], woa_ref[...]) + _dot(mb_ref[...], wob_ref[...])
        o_ref[...] = x
        h_ref[...] = _rms_rows(x, g_ref[...]).astype(BF16)

    h = h_ref[...]
    a = _dot(h, wa_ref[...])
    b = _dot(h, wb_ref[...])
    act = a * (0.5 * _sigmoid(a)) * b
    o_ref[...] += _dot(act.astype(BF16), wo_ref[...])


def _ffn(x, norm_g, w_in, w_out, mix=None, *, tm, fc):
    rows, d = x.shape
    f = w_out.shape[0]
    nj = f // fc
    row_spec = pl.BlockSpec((tm, d), lambda i, j: (i, 0))
    in_specs = [row_spec]
    args = [x]
    if mix is not None:
        ma, mb, woa, wob = mix
        in_specs += [pl.BlockSpec((tm, ma.shape[1]), lambda i, j: (i, 0)),
                     pl.BlockSpec((tm, mb.shape[1]), lambda i, j: (i, 0)),
                     pl.BlockSpec(woa.shape, lambda i, j: (0, 0)),
                     pl.BlockSpec(wob.shape, lambda i, j: (0, 0))]
        args += [ma, mb, woa, wob]
    in_specs += [pl.BlockSpec((1, d), lambda i, j: (0, 0)),
                 pl.BlockSpec((d, fc), lambda i, j: (0, j)),
                 pl.BlockSpec((d, fc), lambda i, j: (0, j + nj)),
                 pl.BlockSpec((fc, d), lambda i, j: (j, 0))]
    args += [norm_g.reshape(1, d), w_in, w_in, w_out]
    return pl.pallas_call(
        functools.partial(_ffn_kernel, with_mix=mix is not None),
        grid=(rows // tm, nj),
        in_specs=in_specs,
        out_specs=row_spec,
        out_shape=jax.ShapeDtypeStruct((rows, d), F32),
        scratch_shapes=[pltpu.VMEM((tm, d), BF16)],
        compiler_params=_params(2),
        name="ffn_mix" if mix is not None else "ffn",
    )(*args)


def _proj_kernel(x_ref, g_ref, w_ref, qg_ref, kg_ref, lb_ref, hm_ref,
                 kf_ref, vf_ref, qb_ref, kb_ref, vb_ref,
                 qh_ref, kin_ref, lf_ref, hi_ref, gt_ref):
    h = _rms_rows(x_ref[...], g_ref[...]).astype(BF16)
    w = SB_WIDTH

    def col(c):
        return _dot(h, w_ref[:, c * w:(c + 1) * w])

    def head_norm(a, gain):
        ms = _dot((a * a).astype(BF16), hm_ref[...])
        return a * lax.rsqrt(ms + EPS) * gain

    qn = head_norm(col(0), qg_ref[...])
    qb_ref[...] = (qn * SB_SCALE).astype(BF16)
    kn = head_norm(col(1), kg_ref[...])
    kf_ref[...] = kn
    kb_ref[...] = kn.astype(BF16)
    sv = col(2)
    vf_ref[...] = sv
    vb_ref[...] = sv.astype(BF16)

    hq = col(3)
    qh_ref[...] = hq * _sigmoid(hq)
    hf = col(4)
    lb = lb_ref[...]
    f = lb + (1.0 - lb) * _sigmoid(hf)
    lf_ref[...] = jnp.log(f)
    kin_ref[...] = (1.0 - lb) * _sigmoid(-hf)
    hi_ref[...] = col(5).astype(BF16)
    hg = col(6)
    gt_ref[...] = hg * _sigmoid(hg)


def _proj(x, norm_g, w_in, q_gain, k_gain, lb, head_mean, *, tm):
    rows, d = x.shape
    w = SB_WIDTH
    row_spec = pl.BlockSpec((tm, w), lambda i: (i, 0))
    vec = lambda n: pl.BlockSpec((1, n), lambda i: (0, 0))
    f32o = jax.ShapeDtypeStruct((rows, w), F32)
    bf16o = jax.ShapeDtypeStruct((rows, w), BF16)
    return pl.pallas_call(
        _proj_kernel,
        grid=(rows // tm,),
        in_specs=[pl.BlockSpec((tm, d), lambda i: (i, 0)), vec(d),
                  pl.BlockSpec(w_in.shape, lambda i: (0, 0)),
                  vec(w), vec(w), vec(w),
                  pl.BlockSpec(head_mean.shape, lambda i: (0, 0))],
        out_specs=[row_spec] * 10,
        out_shape=[f32o, f32o, bf16o, bf16o, bf16o, f32o, f32o, f32o, bf16o, f32o],
        compiler_params=_params(1),
        name="proj",
    )(x, norm_g.reshape(1, d), w_in,
      jnp.tile(q_gain, SB_HEADS).reshape(1, w), jnp.tile(k_gain, SB_HEADS).reshape(1, w),
      lb.reshape(1, w), head_mean)


def _sb_scores(qm, kblk, mask):
    z = _dot_nt(qm, kblk)
    soft = jnp.log(1.0 + jnp.exp(-jnp.abs(z)))
    log_beta = jnp.minimum(z, 0.0) - soft
    log_keep = log_beta - z
    if mask is not None:
        log_keep = jnp.where(mask, log_keep, 0.0)
    return log_beta, log_keep


def _sb_tail(log_keep, upper):
    hi = log_keep.astype(BF16)
    lo = (log_keep - hi.astype(F32)).astype(BF16)
    tail = _dot(hi, upper) + _dot(lo, upper)
    total = tail[:, :1] + log_keep[:, :1]
    return tail, total


def _sb_weights(log_beta, tail, carry, mask):
    w = jnp.exp(log_beta + tail + carry)
    if mask is not None:
        w = jnp.where(mask, w, 0.0)
    return w.astype(BF16)


def _sb_finish(acc0, acc1, gain):
    lane = lax.broadcasted_iota(jnp.int32, (1, LANES), 1)
    first = lane < SB_HEAD_DIM
    o = jnp.where(first, acc0, acc1)
    sq = o * o
    s0 = jnp.sum(jnp.where(first, sq, 0.0), axis=-1, keepdims=True)
    s1 = jnp.sum(jnp.where(first, 0.0, sq), axis=-1, keepdims=True)
    ms = jnp.where(first, s0, s1) * (1.0 / SB_HEAD_DIM)
    return o * lax.rsqrt(ms + EPS) * gain


def _head_masked(q):
    lane = lax.broadcasted_iota(jnp.int32, (1, LANES), 1)
    first = lane < SB_HEAD_DIM
    zero = jnp.zeros_like(q)
    return jnp.where(first, q, zero), jnp.where(first, zero, q)


def _sb_prompt_kernel(q_ref, k_ref, v_ref, gain_ref, up_ref, o_ref, *, tq):
    i = pl.program_id(2)
    qms = _head_masked(q_ref[...])
    upper = up_ref[...]
    row = lax.broadcasted_iota(jnp.int32, (tq, tq), 0)
    colid = lax.broadcasted_iota(jnp.int32, (tq, tq), 1)
    causal = colid < row

    def tile(qm, kblk, vblk, carry, acc, mask):
        log_beta, log_keep = _sb_scores(qm, kblk, mask)
        tail, total = _sb_tail(log_keep, upper)
        w = _sb_weights(log_beta, tail, carry, mask)
        return carry + total, acc + _dot(w, vblk)

    def block(kb, state, mask):
        start = pl.multiple_of(kb * tq, tq)
        kblk = k_ref[pl.ds(start, tq), :]
        vblk = v_ref[pl.ds(start, tq), :]
        c0, a0, c1, a1 = state
        c0, a0 = tile(qms[0], kblk, vblk, c0, a0, mask)
        c1, a1 = tile(qms[1], kblk, vblk, c1, a1, mask)
        return c0, a0, c1, a1

    zc = jnp.zeros((tq, 1), F32)
    za = jnp.zeros((tq, LANES), F32)
    state = block(i, (zc, za, zc, za), causal)
    state = lax.fori_loop(0, i, lambda j, s: block(i - 1 - j, s, None), state)
    o_ref[...] = _sb_finish(state[1], state[3], gain_ref[...]).astype(o_ref.dtype)


def _sb_prompt(q, k, v, gain_pair, upper, *, batch, tq):
    rows, w = q.shape
    t = rows // batch
    nq = t // tq
    qspec = pl.BlockSpec((tq, LANES), lambda b, p, i: (b * nq + i, p))
    kvspec = pl.BlockSpec((t, LANES), lambda b, p, i: (b, p))
    return pl.pallas_call(
        functools.partial(_sb_prompt_kernel, tq=tq),
        grid=(batch, HEAD_PAIRS, nq),
        in_specs=[qspec, kvspec, kvspec,
                  pl.BlockSpec((1, LANES), lambda b, p, i: (0, 0)),
                  pl.BlockSpec((tq, tq), lambda b, p, i: (0, 0))],
        out_specs=qspec,
        out_shape=jax.ShapeDtypeStruct((rows, w), BF16),
        compiler_params=_params(3),
        name="sb_prompt",
    )(q, k, v, gain_pair, upper)


def _sb_sample_kernel(q_ref, kn_ref, vn_ref, ck_ref, cv_ref, gain_ref, up_ref, upn_ref,
                      o_ref, *, tk, n_blocks, t_new):
    qms = _head_masked(q_ref[...])
    upper = up_ref[...]
    row = lax.broadcasted_iota(jnp.int32, (t_new, t_new), 0)
    colid = lax.broadcasted_iota(jnp.int32, (t_new, t_new), 1)
    causal = colid < row
    kn = kn_ref[...]
    vn = vn_ref[...]

    accs = []
    for qm in qms:
        log_beta, log_keep = _sb_scores(qm, kn, causal)
        tail, carry = _sb_tail(log_keep, upn_ref[...])
        w = _sb_weights(log_beta, tail, jnp.zeros((t_new, 1), F32), causal)
        acc = _dot(w, vn)
        pending = []
        for kb in range(n_blocks - 1, -1, -1):
            kblk = ck_ref[kb * tk:(kb + 1) * tk, :].astype(BF16)
            log_beta, log_keep = _sb_scores(qm, kblk, None)
            tail, total = _sb_tail(log_keep, upper)
            pending.append((kb, log_beta, tail, total))
        for kb, log_beta, tail, total in pending:
            w = _sb_weights(log_beta, tail, carry, None)
            acc = acc + _dot(w, cv_ref[kb * tk:(kb + 1) * tk, :].astype(BF16))
            carry = carry + total
        accs.append(acc)
    o_ref[...] = _sb_finish(accs[0], accs[1], gain_ref[...]).astype(o_ref.dtype)


def _sb_sample(q, k_new, v_new, cache_k, cache_v, gain_pair, upper, upper_new, *, batch, tk):
    rows, w = q.shape
    t_new = rows // batch
    past = cache_k.shape[0] // batch
    qspec = pl.BlockSpec((t_new, LANES), lambda b, p: (b, p))
    cspec = pl.BlockSpec((past, LANES), lambda b, p: (b, p))
    return pl.pallas_call(
        functools.partial(_sb_sample_kernel, tk=tk, n_blocks=past // tk, t_new=t_new),
        grid=(batch, HEAD_PAIRS),
        in_specs=[qspec, qspec, qspec, cspec, cspec,
                  pl.BlockSpec((1, LANES), lambda b, p: (0, 0)),
                  pl.BlockSpec((tk, tk), lambda b, p: (0, 0)),
                  pl.BlockSpec((t_new, t_new), lambda b, p: (0, 0))],
        out_specs=qspec,
        out_shape=jax.ShapeDtypeStruct((rows, w), BF16),
        compiler_params=_params(2),
        name="sb_sample",
    )(q, k_new, v_new, cache_k, cache_v, gain_pair, upper, upper_new)


def _hgrn_chunk(q, k, g, v, st, lower, ones, c):
    n = c // SUB
    g1 = g.astype(BF16)
    r1 = g - g1.astype(F32)
    g2 = r1.astype(BF16)
    g3 = (r1 - g2.astype(F32)).astype(BF16)
    bs = _dot(lower, jnp.concatenate([g1, g2, g3], axis=1))
    b = bs[:, :HG_DIM] + (bs[:, HG_DIM:2 * HG_DIM] + bs[:, 2 * HG_DIM:])
    b_last = b[c - 1:c, :]

    b3 = b.reshape(n, SUB, HG_DIM)
    q3 = q.reshape(n, SUB, HG_DIM)
    k3 = k.reshape(n, SUB, HG_DIM)
    kt = (k3 * jnp.exp(b3[:, SUB - 1:SUB, :] - b3)).reshape(c, HG_DIM).astype(BF16)

    row = lax.broadcasted_iota(jnp.int32, (c, 1), 0)
    a = None
    if n > 1:
        qs, ks = [], []
        zero = jnp.zeros_like(kt)
        for j in range(n - 1):
            ref = b[SUB * j + SUB - 1:SUB * j + SUB, :]
            qj = q * jnp.exp(jnp.minimum(b - ref, 0.0))
            qs.append(jnp.where(row >= SUB * (j + 1), qj, 0.0).astype(BF16))
            ks.append(jnp.where((row >= SUB * j) & (row < SUB * (j + 1)), kt, zero))
        a = _dot_nt(jnp.concatenate(qs, axis=1), jnp.concatenate(ks, axis=1))

    sub = lax.broadcasted_iota(jnp.int32, (1, SUB, 1), 1)
    ps = []
    for s in range(SUB):
        e = jnp.exp(jnp.minimum(b3 - b3[:, s:s + 1, :], 0.0))
        p = q3 * e * k3[:, s:s + 1, :]
        ps.append(jnp.where(sub >= s, p, 0.0).reshape(c, HG_DIM).astype(BF16))
    sums = _dot(jnp.concatenate(ps, axis=0), ones)
    colid = lax.broadcasted_iota(jnp.int32, (c, c), 1)
    base = (lax.broadcasted_iota(jnp.int32, (c, c), 0) // SUB) * SUB
    for s in range(SUB):
        d = jnp.where(colid == base + s, sums[s * c:(s + 1) * c, :], 0.0)
        a = d if a is None else a + d

    o = _dot_nt((q * jnp.exp(b)).astype(BF16), st.astype(BF16)) + _dot(a.astype(BF16), v)
    khat = (k * jnp.exp(b_last - b)).astype(BF16)
    st_new = st * jnp.exp(b_last) + _dot_tn(v, khat)
    return o, st_new


def _hgrn_kernel(*refs, c, n_chunks, has_s0):
    if has_s0:
        (qh_ref, kin_ref, lf_ref, hi_ref, gt_ref, gain_ref, low_ref, ones_ref, s0_ref,
         o_ref, s_ref, st_ref) = refs
    else:
        (qh_ref, kin_ref, lf_ref, hi_ref, gt_ref, gain_ref, low_ref, ones_ref,
         o_ref, s_ref, st_ref) = refs
    step = pl.program_id(1)

    @pl.when(step == 0)
    def _():
        for h in range(HG_HEADS):
            st_ref[h] = s0_ref[0, h].T if has_s0 else jnp.zeros((HG_DIM, HG_DIM), F32)

    lower = low_ref[...]
    ones = ones_ref[...]
    gain = gain_ref[...]

    def chunk(ci, carry):
        r0 = pl.multiple_of(ci * c, c)
        for h in range(HG_HEADS):
            ln = slice(h * HG_DIM, (h + 1) * HG_DIM)
            o, st = _hgrn_chunk(qh_ref[pl.ds(r0, c), ln], kin_ref[pl.ds(r0, c), ln],
                                lf_ref[pl.ds(r0, c), ln], hi_ref[pl.ds(r0, c), ln],
                                st_ref[h], lower, ones, c)
            st_ref[h] = st
            y = _rms_rows(o, gain) * gt_ref[pl.ds(r0, c), ln]
            o_ref[pl.ds(r0, c), ln] = y.astype(o_ref.dtype)
        return carry

    lax.fori_loop(0, n_chunks, chunk, 0)

    @pl.when(step == pl.num_programs(1) - 1)
    def _():
        for h in range(HG_HEADS):
            s_ref[0, h] = st_ref[h].T


def _hgrn(qh, kin, lf, hi, gate, gain, s0, *, batch, c, rb):
    rows, w = qh.shape
    t = rows // batch
    nsteps = t // rb
    row_spec = pl.BlockSpec((rb, w), lambda b, s: (b * nsteps + s, 0))
    lower = jnp.tril(jnp.ones((c, c), BF16))
    ones = jnp.ones((HG_DIM, c), BF16)
    state_spec = pl.BlockSpec((1, HG_HEADS, HG_DIM, HG_DIM), lambda b, s: (b, 0, 0, 0))
    in_specs = [row_spec] * 5 + [pl.BlockSpec((1, HG_DIM), lambda b, s: (0, 0)),
                                 pl.BlockSpec((c, c), lambda b, s: (0, 0)),
                                 pl.BlockSpec((HG_DIM, c), lambda b, s: (0, 0))]
    args = [qh, kin, lf, hi, gate, gain.reshape(1, HG_DIM), lower, ones]
    if s0 is not None:
        in_specs.append(state_spec)
        args.append(s0)
    return pl.pallas_call(
        functools.partial(_hgrn_kernel, c=c, n_chunks=rb // c, has_s0=s0 is not None),
        grid=(batch, nsteps),
        in_specs=in_specs,
        out_specs=[row_spec, state_spec],
        out_shape=[jax.ShapeDtypeStruct((rows, w), BF16),
                   jax.ShapeDtypeStruct((batch, HG_HEADS, HG_DIM, HG_DIM), F32)],
        scratch_shapes=[pltpu.VMEM((HG_HEADS, HG_DIM, HG_DIM), F32)],
        compiler_params=_params(2),
        name="hgrn_s0" if s0 is not None else "hgrn",
    )(*args)


def _tile_sizes(rows, t):
    tm = min(1024, rows)
    tq = min(256, t)
    c = min(64, t)
    rb = min(512, t)
    return tm, tq, c, rb


def _strict_upper(n):
    r = lax.broadcasted_iota(jnp.int32, (n, n), 0)
    col = lax.broadcasted_iota(jnp.int32, (n, n), 1)
    return (r > col).astype(BF16)


def _layer(x, wts, lb, *, batch, cache=None, s0=None):
    rows, _ = x.shape
    t = rows // batch
    tm, tq, c, rb = _tile_sizes(rows, t)
    fc = 256
    x1 = _ffn(x, wts["ffn1_norm"], wts["ffn1_w_in"], wts["ffn1_w_out"], tm=tm, fc=fc)
    (kf, vf, qb, kb, vb, qh, kin, lf, hi, gate) = _proj(
        x1, wts["mix_norm"], wts["w_in"], wts["sb_q_gain"], wts["sb_k_gain"], lb,
        wts["head_mean"], tm=min(512, rows))
    gain_pair = jnp.tile(wts["sb_out_gain"], LANES // SB_HEAD_DIM).reshape(1, LANES)
    if cache is None:
        sb_o = _sb_prompt(qb, kb, vb, gain_pair, _strict_upper(tq), batch=batch, tq=tq)
    else:
        tk = min(256, cache[0].shape[0] // batch)
        sb_o = _sb_sample(qb, kb, vb, cache[0], cache[1], gain_pair, _strict_upper(tk),
                          _strict_upper(t), batch=batch, tk=tk)
    hg_o, state = _hgrn(qh, kin, lf, hi, gate, wts["hg_out_gain"], s0, batch=batch, c=c, rb=rb)
    w_mix = wts["w_out"]
    x3 = _ffn(x1, wts["ffn2_norm"], wts["ffn2_w_in"], wts["ffn2_w_out"],
              mix=(sb_o, hg_o, w_mix[:SB_WIDTH], w_mix[SB_WIDTH:]), tm=tm, fc=fc)
    return x3, kf, vf, state


def kernel(x_prompt, x_sample, cache_sb_k, cache_sb_v, state_hgrn, ffn1_norm, ffn1_w_in,
           ffn1_w_out, mix_norm, w_in, sb_q_gain, sb_k_gain, hg_lb_logits, sb_out_gain,
           hg_out_gain, w_out, ffn2_norm, ffn2_w_in, ffn2_w_out):
    depth = w_in.shape[0]
    bp, tp, d = x_prompt.shape
    bs, ts, _ = x_sample.shape
    past = cache_sb_k.shape[2]
    lb_all = jnp.cumsum(jax.nn.softmax(hg_lb_logits.astype(F32), axis=0), axis=0)
    hid = lax.broadcasted_iota(jnp.int32, (SB_WIDTH, SB_WIDTH), 0) // SB_HEAD_DIM
    hid_t = lax.broadcasted_iota(jnp.int32, (SB_WIDTH, SB_WIDTH), 1) // SB_HEAD_DIM
    head_mean = jnp.where(hid == hid_t, 1.0 / SB_HEAD_DIM, 0.0).astype(BF16)

    xp = x_prompt.reshape(bp * tp, d)
    xs = x_sample.reshape(bs * ts, d)
    outs = [[] for _ in range(6)]
    for l in range(depth):
        wts = dict(
            ffn1_norm=ffn1_norm[l], ffn1_w_in=ffn1_w_in[l].astype(BF16),
            ffn1_w_out=ffn1_w_out[l].astype(BF16), mix_norm=mix_norm[l],
            w_in=w_in[l].astype(BF16), sb_q_gain=sb_q_gain[l], sb_k_gain=sb_k_gain[l],
            sb_out_gain=sb_out_gain[l], hg_out_gain=hg_out_gain[l],
            w_out=w_out[l].astype(BF16), ffn2_norm=ffn2_norm[l],
            ffn2_w_in=ffn2_w_in[l].astype(BF16), ffn2_w_out=ffn2_w_out[l].astype(BF16),
            head_mean=head_mean)
        xp, kp, vp, sp = _layer(xp, wts, lb_all[l], batch=bp)
        cache = (cache_sb_k[l].reshape(bs * past, SB_WIDTH),
                 cache_sb_v[l].reshape(bs * past, SB_WIDTH))
        xs, kn, vn, sn = _layer(xs, wts, lb_all[l], batch=bs, cache=cache, s0=state_hgrn[l])
        hd = (SB_HEADS, SB_HEAD_DIM)
        for lst, val in zip(outs, (kp.reshape(bp, tp, *hd), vp.reshape(bp, tp, *hd), sp,
                                   kn.reshape(bs, ts, *hd), vn.reshape(bs, ts, *hd), sn)):
            lst.append(val)
    return (xp.reshape(bp, tp, d), xs.reshape(bs, ts, d)) + tuple(
        jnp.stack(lst, axis=0) for lst in outs)
```

```python
import functools

import jax
import jax.numpy as jnp
from jax import lax
from jax.experimental import pallas as pl
from jax.experimental.pallas import tpu as pltpu

EPS = 1e-6
SB_HEADS = 8
SB_HEAD_DIM = 64
SB_WIDTH = SB_HEADS * SB_HEAD_DIM
SB_SCALE = SB_HEAD_DIM ** -0.5
LOG2_E = 1.4426950408889634
EXP2_CLAMP = 126.0
HG_HEADS = 4
HG_DIM = 128
HG_WIDTH = HG_HEADS * HG_DIM
LANES = 128
SUB = 8
HEAD_PAIRS = SB_WIDTH // LANES

BF16 = jnp.bfloat16
F32 = jnp.float32

VMEM_LIMIT = 56 * 1024 * 1024


def _dot(a, b):
    return jnp.dot(a, b, preferred_element_type=F32)


def _dot_nt(a, b):
    return lax.dot_general(a, b, (((1,), (1,)), ((), ())), preferred_element_type=F32)


def _dot_tn(a, b):
    return lax.dot_general(a, b, (((0,), (0,)), ((), ())), preferred_element_type=F32)


def _sigmoid(x):
    return 1.0 / (1.0 + jnp.exp(-x))


def _rms_rows(x, g):
    ms = jnp.mean(x * x, axis=-1, keepdims=True)
    return x * lax.rsqrt(ms + EPS) * g


def _params(n_grid):
    return pltpu.CompilerParams(
        dimension_semantics=("arbitrary",) * n_grid, vmem_limit_bytes=VMEM_LIMIT)


def _ffn_kernel(*refs, with_mix):
    if with_mix:
        (x_ref, ma_ref, mb_ref, woa_ref, wob_ref, g_ref, wa_ref, wb_ref, wo_ref,
         o_ref, h_ref) = refs
    else:
        x_ref, g_ref, wa_ref, wb_ref, wo_ref, o_ref, h_ref = refs

    @pl.when(pl.program_id(1) == 0)
    def _():
        x = x_ref[...]
        if with_mix:
            x = x + _dot(ma_ref[...], woa_ref[...]) + _dot(mb_ref[...], wob_ref[...])
        o_ref[...] = x
        h_ref[...] = _rms_rows(x, g_ref[...]).astype(BF16)

    h = h_ref[...]
    a = _dot(h, wa_ref[...])
    b = _dot(h, wb_ref[...])
    act = a * (0.5 * _sigmoid(a)) * b
    o_ref[...] += _dot(act.astype(BF16), wo_ref[...])


def _ffn(x, norm_g, w_in, w_out, mix=None, *, tm, fc):
    rows, d = x.shape
    f = w_out.shape[0]
    nj = f // fc
    row_spec = pl.BlockSpec((tm, d), lambda i, j: (i, 0))
    in_specs = [row_spec]
    args = [x]
    if mix is not None:
        ma, mb, woa, wob = mix
        in_specs += [pl.BlockSpec((tm, ma.shape[1]), lambda i, j: (i, 0)),
                     pl.BlockSpec((tm, mb.shape[1]), lambda i, j: (i, 0)),
                     pl.BlockSpec(woa.shape, lambda i, j: (0, 0)),
                     pl.BlockSpec(wob.shape, lambda i, j: (0, 0))]
        args += [ma, mb, woa, wob]
    in_specs += [pl.BlockSpec((1, d), lambda i, j: (0, 0)),
                 pl.BlockSpec((d, fc), lambda i, j: (0, j)),
                 pl.BlockSpec((d, fc), lambda i, j: (0, j + nj)),
                 pl.BlockSpec((fc, d), lambda i, j: (j, 0))]
    args += [norm_g.reshape(1, d), w_in, w_in, w_out]
    return pl.pallas_call(
        functools.partial(_ffn_kernel, with_mix=mix is not None),
        grid=(rows // tm, nj),
        in_specs=in_specs,
        out_specs=row_spec,
        out_shape=jax.ShapeDtypeStruct((rows, d), F32),
        scratch_shapes=[pltpu.VMEM((tm, d), BF16)],
        compiler_params=_params(2),
        name="ffn_mix" if mix is not None else "ffn",
    )(*args)


def _proj_kernel(x_ref, g_ref, w_ref, qg_ref, kg_ref, lb_ref, hm_ref,
                 kf_ref, vf_ref, qb_ref, kb_ref, vb_ref,
                 qh_ref, kin_ref, lf_ref, hi_ref, gt_ref):
    h = _rms_rows(x_ref[...], g_ref[...]).astype(BF16)
    w = SB_WIDTH

    def col(c):
        return _dot(h, w_ref[:, c * w:(c + 1) * w])

    def head_norm(a, gain):
        ms = _dot((a * a).astype(BF16), hm_ref[...])
        return a * lax.rsqrt(ms + EPS) * gain

    qn = head_norm(col(0), qg_ref[...])
    qb_ref[...] = (qn * (SB_SCALE * LOG2_E)).astype(BF16)
    kn = head_norm(col(1), kg_ref[...])
    kf_ref[...] = kn
    kb_ref[...] = kn.astype(BF16)
    sv = col(2)
    vf_ref[...] = sv
    vb_ref[...] = sv.astype(BF16)

    hq = col(3)
    qh_ref[...] = hq * _sigmoid(hq)
    hf = col(4)
    lb = lb_ref[...]
    f = lb + (1.0 - lb) * _sigmoid(hf)
    lf_ref[...] = jnp.log(f)
    kin_ref[...] = (1.0 - lb) * _sigmoid(-hf)
    hi_ref[...] = col(5).astype(BF16)
    hg = col(6)
    gt_ref[...] = hg * _sigmoid(hg)


def _proj(x, norm_g, w_in, q_gain, k_gain, lb, head_mean, *, tm):
    rows, d = x.shape
    w = SB_WIDTH
    row_spec = pl.BlockSpec((tm, w), lambda i: (i, 0))
    vec = lambda n: pl.BlockSpec((1, n), lambda i: (0, 0))
    f32o = jax.ShapeDtypeStruct((rows, w), F32)
    bf16o = jax.ShapeDtypeStruct((rows, w), BF16)
    return pl.pallas_call(
        _proj_kernel,
        grid=(rows // tm,),
        in_specs=[pl.BlockSpec((tm, d), lambda i: (i, 0)), vec(d),
                  pl.BlockSpec(w_in.shape, lambda i: (0, 0)),
                  vec(w), vec(w), vec(w),
                  pl.BlockSpec(head_mean.shape, lambda i: (0, 0))],
        out_specs=[row_spec] * 10,
        out_shape=[f32o, f32o, bf16o, bf16o, bf16o, f32o, f32o, f32o, bf16o, f32o],
        compiler_params=_params(1),
        name="proj",
    )(x, norm_g.reshape(1, d), w_in,
      jnp.tile(q_gain, SB_HEADS).reshape(1, w), jnp.tile(k_gain, SB_HEADS).reshape(1, w),
      lb.reshape(1, w), head_mean)


def _sb_tiles(ops, upper2, mask):
    zs = [_dot_nt(q, kblk) for q, kblk, _ in ops]
    suffixes = []
    for z in zs:
        drop = jnp.maximum(z, jnp.log2(1.0 + jnp.exp2(jnp.minimum(z, EXP2_CLAMP))))
        if mask is not None:
            drop = jnp.where(mask, drop, 0.0)
        hi = drop.astype(BF16)
        lo = (drop - hi.astype(F32)).astype(BF16)
        suffixes.append(_dot(jnp.concatenate([hi, lo], axis=1), upper2))
    out = []
    for z, suffix, (_, _, vblk) in zip(zs, suffixes, ops):
        w = jnp.exp2(z - suffix)
        if mask is not None:
            w = jnp.where(mask, w, 0.0)
        out.append((_dot(w.astype(BF16), vblk), suffix[:, :1]))
    return out


def _sb_finish(acc0, acc1, gain):
    lane = lax.broadcasted_iota(jnp.int32, (1, LANES), 1)
    first = lane < SB_HEAD_DIM
    o = jnp.where(first, acc0, acc1)
    sq = o * o
    s0 = jnp.sum(jnp.where(first, sq, 0.0), axis=-1, keepdims=True)
    s1 = jnp.sum(jnp.where(first, 0.0, sq), axis=-1, keepdims=True)
    ms = jnp.where(first, s0, s1) * (1.0 / SB_HEAD_DIM)
    return o * lax.rsqrt(ms + EPS) * gain


def _head_masked(q):
    lane = lax.broadcasted_iota(jnp.int32, (1, LANES), 1)
    first = lane < SB_HEAD_DIM
    zero = jnp.zeros_like(q)
    return jnp.where(first, q, zero), jnp.where(first, zero, q)


def _sb_prompt_kernel(q_ref, k_ref, v_ref, gain_ref, up_ref, o_ref, acc_ref, car_ref, *, tq):
    i = pl.program_id(2)
    qms = _head_masked(q_ref[...])
    upper = up_ref[...]
    row = lax.broadcasted_iota(jnp.int32, (tq, tq), 0)
    colid = lax.broadcasted_iota(jnp.int32, (tq, tq), 1)
    causal = colid < row

    def blocks(kbs, mask):
        ops = []
        for kb in kbs:
            start = pl.multiple_of(kb * tq, tq)
            kblk = k_ref[pl.ds(start, tq), :]
            vblk = v_ref[pl.ds(start, tq), :]
            ops += [(qms[0], kblk, vblk), (qms[1], kblk, vblk)]
        res = _sb_tiles(ops, upper, mask)
        for h in range(2):
            carry = car_ref[h]
            acc = acc_ref[h]
            for pv, total in res[h::2]:
                acc = acc + jnp.exp2(-carry) * pv
                carry = carry + total
            car_ref[h] = carry
            acc_ref[h] = acc

    acc_ref[...] = jnp.zeros_like(acc_ref)
    car_ref[...] = jnp.zeros_like(car_ref)
    blocks([i], causal)

    @pl.when(i % 2 == 1)
    def _():
        blocks([i - 1], None)

    n_pairs = i // 2

    def pair(j, carry):
        top = 2 * (n_pairs - j) - 1
        blocks([top, top - 1], None)
        return carry

    lax.fori_loop(0, n_pairs, pair, 0)
    o_ref[...] = _sb_finish(acc_ref[0], acc_ref[1], gain_ref[...]).astype(o_ref.dtype)


def _sb_prompt(q, k, v, gain_pair, upper, *, batch, tq):
    rows, w = q.shape
    t = rows // batch
    nq = t // tq
    qspec = pl.BlockSpec((tq, LANES), lambda b, p, i: (b * nq + i, p))
    kvspec = pl.BlockSpec((t, LANES), lambda b, p, i: (b, p))
    return pl.pallas_call(
        functools.partial(_sb_prompt_kernel, tq=tq),
        grid=(batch, HEAD_PAIRS, nq),
        in_specs=[qspec, kvspec, kvspec,
                  pl.BlockSpec((1, LANES), lambda b, p, i: (0, 0)),
                  pl.BlockSpec((2 * tq, tq), lambda b, p, i: (0, 0))],
        out_specs=qspec,
        out_shape=jax.ShapeDtypeStruct((rows, w), BF16),
        scratch_shapes=[pltpu.VMEM((2, tq, LANES), F32), pltpu.VMEM((2, tq, 1), F32)],
        compiler_params=_params(3),
        name="sb_prompt",
    )(q, k, v, gain_pair, upper)


def _sb_sample_kernel(q_ref, kn_ref, vn_ref, ck_ref, cv_ref, gain_ref, up_ref, upn_ref,
                      o_ref, *, tk, n_blocks, t_new):
    qms = _head_masked(q_ref[...])
    upper = up_ref[...]
    row = lax.broadcasted_iota(jnp.int32, (t_new, t_new), 0)
    colid = lax.broadcasted_iota(jnp.int32, (t_new, t_new), 1)
    causal = colid < row
    kn = kn_ref[...]
    vn = vn_ref[...]

    res = _sb_tiles([(qm, kn, vn) for qm in qms], upn_ref[...], causal)
    accs = [r[0] for r in res]
    carries = [r[1] for r in res]
    ops = []
    for kb in range(n_blocks - 1, -1, -1):
        rows = slice(kb * tk, (kb + 1) * tk)
        kblk = ck_ref[rows, :].astype(BF16)
        vblk = cv_ref[rows, :].astype(BF16)
        ops += [(qm, kblk, vblk) for qm in qms]
    for n, (pv, total) in enumerate(_sb_tiles(ops, upper, None)):
        h = n % 2
        accs[h] = accs[h] + jnp.exp2(-carries[h]) * pv
        carries[h] = carries[h] + total
    o_ref[...] = _sb_finish(accs[0], accs[1], gain_ref[...]).astype(o_ref.dtype)


def _sb_sample(q, k_new, v_new, cache_k, cache_v, gain_pair, upper, upper_new, *, batch, tk):
    rows, w = q.shape
    t_new = rows // batch
    past = cache_k.shape[0] // batch
    qspec = pl.BlockSpec((t_new, LANES), lambda b, p: (b, p))
    cspec = pl.BlockSpec((past, LANES), lambda b, p: (b, p))
    return pl.pallas_call(
        functools.partial(_sb_sample_kernel, tk=tk, n_blocks=past // tk, t_new=t_new),
        grid=(batch, HEAD_PAIRS),
        in_specs=[qspec, qspec, qspec, cspec, cspec,
                  pl.BlockSpec((1, LANES), lambda b, p: (0, 0)),
                  pl.BlockSpec((2 * tk, tk), lambda b, p: (0, 0)),
                  pl.BlockSpec((2 * t_new, t_new), lambda b, p: (0, 0))],
        out_specs=qspec,
        out_shape=jax.ShapeDtypeStruct((rows, w), BF16),
        compiler_params=_params(2),
        name="sb_sample",
    )(q, k_new, v_new, cache_k, cache_v, gain_pair, upper, upper_new)


def _hgrn_chunk(q, k, g, v, st, lower, ones, c):
    n = c // SUB
    g1 = g.astype(BF16)
    r1 = g - g1.astype(F32)
    g2 = r1.astype(BF16)
    g3 = (r1 - g2.astype(F32)).astype(BF16)
    bs = _dot(lower, jnp.concatenate([g1, g2, g3], axis=1))
    b = bs[:, :HG_DIM] + (bs[:, HG_DIM:2 * HG_DIM] + bs[:, 2 * HG_DIM:])
    b_last = b[c - 1:c, :]

    b3 = b.reshape(n, SUB, HG_DIM)
    q3 = q.reshape(n, SUB, HG_DIM)
    k3 = k.reshape(n, SUB, HG_DIM)
    kt = (k3 * jnp.exp(b3[:, SUB - 1:SUB, :] - b3)).reshape(c, HG_DIM).astype(BF16)

    row = lax.broadcasted_iota(jnp.int32, (c, 1), 0)
    a = None
    if n > 1:
        qs, ks = [], []
        zero = jnp.zeros_like(kt)
        for j in range(n - 1):
            ref = b[SUB * j + SUB - 1:SUB * j + SUB, :]
            qj = q * jnp.exp(jnp.minimum(b - ref, 0.0))
            qs.append(jnp.where(row >= SUB * (j + 1), qj, 0.0).astype(BF16))
            ks.append(jnp.where((row >= SUB * j) & (row < SUB * (j + 1)), kt, zero))
        a = _dot_nt(jnp.concatenate(qs, axis=1), jnp.concatenate(ks, axis=1))

    sub = lax.broadcasted_iota(jnp.int32, (1, SUB, 1), 1)
    ps = []
    for s in range(SUB):
        e = jnp.exp(jnp.minimum(b3 - b3[:, s:s + 1, :], 0.0))
        p = q3 * e * k3[:, s:s + 1, :]
        ps.append(jnp.where(sub >= s, p, 0.0).reshape(c, HG_DIM).astype(BF16))
    sums = _dot(jnp.concatenate(ps, axis=0), ones)
    colid = lax.broadcasted_iota(jnp.int32, (c, c), 1)
    base = (lax.broadcasted_iota(jnp.int32, (c, c), 0) // SUB) * SUB
    for s in range(SUB):
        d = jnp.where(colid == base + s, sums[s * c:(s + 1) * c, :], 0.0)
        a = d if a is None else a + d

    o = _dot_nt((q * jnp.exp(b)).astype(BF16), st.astype(BF16)) + _dot(a.astype(BF16), v)
    khat = (k * jnp.exp(b_last - b)).astype(BF16)
    st_new = st * jnp.exp(b_last) + _dot_tn(v, khat)
    return o, st_new


def _hgrn_kernel(*refs, c, n_chunks, has_s0):
    if has_s0:
        (qh_ref, kin_ref, lf_ref, hi_ref, gt_ref, gain_ref, low_ref, ones_ref, s0_ref,
         o_ref, s_ref, st_ref) = refs
    else:
        (qh_ref, kin_ref, lf_ref, hi_ref, gt_ref, gain_ref, low_ref, ones_ref,
         o_ref, s_ref, st_ref) = refs
    step = pl.program_id(1)

    @pl.when(step == 0)
    def _():
        for h in range(HG_HEADS):
            st_ref[h] = s0_ref[0, h].T if has_s0 else jnp.zeros((HG_DIM, HG_DIM), F32)

    lower = low_ref[...]
    ones = ones_ref[...]
    gain = gain_ref[...]

    def chunk(ci, carry):
        r0 = pl.multiple_of(ci * c, c)
        for h in range(HG_HEADS):
            ln = slice(h * HG_DIM, (h + 1) * HG_DIM)
            o, st = _hgrn_chunk(qh_ref[pl.ds(r0, c), ln], kin_ref[pl.ds(r0, c), ln],
                                lf_ref[pl.ds(r0, c), ln], hi_ref[pl.ds(r0, c), ln],
                                st_ref[h], lower, ones, c)
            st_ref[h] = st
            y = _rms_rows(o, gain) * gt_ref[pl.ds(r0, c), ln]
            o_ref[pl.ds(r0, c), ln] = y.astype(o_ref.dtype)
        return carry

    lax.fori_loop(0, n_chunks, chunk, 0)

    @pl.when(step == pl.num_programs(1) - 1)
    def _():
        for h in range(HG_HEADS):
            s_ref[0, h] = st_ref[h].T


def _hgrn(qh, kin, lf, hi, gate, gain, s0, *, batch, c, rb):
    rows, w = qh.shape
    t = rows // batch
    nsteps = t // rb
    row_spec = pl.BlockSpec((rb, w), lambda b, s: (b * nsteps + s, 0))
    lower = jnp.tril(jnp.ones((c, c), BF16))
    ones = jnp.ones((HG_DIM, c), BF16)
    state_spec = pl.BlockSpec((1, HG_HEADS, HG_DIM, HG_DIM), lambda b, s: (b, 0, 0, 0))
    in_specs = [row_spec] * 5 + [pl.BlockSpec((1, HG_DIM), lambda b, s: (0, 0)),
                                 pl.BlockSpec((c, c), lambda b, s: (0, 0)),
                                 pl.BlockSpec((HG_DIM, c), lambda b, s: (0, 0))]
    args = [qh, kin, lf, hi, gate, gain.reshape(1, HG_DIM), lower, ones]
    if s0 is not None:
        in_specs.append(state_spec)
        args.append(s0)
    return pl.pallas_call(
        functools.partial(_hgrn_kernel, c=c, n_chunks=rb // c, has_s0=s0 is not None),
        grid=(batch, nsteps),
        in_specs=in_specs,
        out_specs=[row_spec, state_spec],
        out_shape=[jax.ShapeDtypeStruct((rows, w), BF16),
                   jax.ShapeDtypeStruct((batch, HG_HEADS, HG_DIM, HG_DIM), F32)],
        scratch_shapes=[pltpu.VMEM((HG_HEADS, HG_DIM, HG_DIM), F32)],
        compiler_params=_params(2),
        name="hgrn_s0" if s0 is not None else "hgrn",
    )(*args)


def _tile_sizes(rows, t):
    tm = min(1024, rows)
    tq = min(256, t)
    c = min(64, t)
    rb = min(512, t)
    return tm, tq, c, rb


def _suffix_ones(n):
    r = lax.broadcasted_iota(jnp.int32, (2 * n, n), 0) % n
    col = lax.broadcasted_iota(jnp.int32, (2 * n, n), 1)
    return (r >= col).astype(BF16)


def _layer(x, wts, lb, *, batch, cache=None, s0=None):
    rows, _ = x.shape
    t = rows // batch
    tm, tq, c, rb = _tile_sizes(rows, t)
    fc = 256
    x1 = _ffn(x, wts["ffn1_norm"], wts["ffn1_w_in"], wts["ffn1_w_out"], tm=tm, fc=fc)
    (kf, vf, qb, kb, vb, qh, kin, lf, hi, gate) = _proj(
        x1, wts["mix_norm"], wts["w_in"], wts["sb_q_gain"], wts["sb_k_gain"], lb,
        wts["head_mean"], tm=min(512, rows))
    gain_pair = jnp.tile(wts["sb_out_gain"], LANES // SB_HEAD_DIM).reshape(1, LANES)
    if cache is None:
        sb_o = _sb_prompt(qb, kb, vb, gain_pair, _suffix_ones(tq), batch=batch, tq=tq)
    else:
        tk = min(256, cache[0].shape[0] // batch)
        sb_o = _sb_sample(qb, kb, vb, cache[0], cache[1], gain_pair, _suffix_ones(tk),
                          _suffix_ones(t), batch=batch, tk=tk)
    hg_o, state = _hgrn(qh, kin, lf, hi, gate, wts["hg_out_gain"], s0, batch=batch, c=c, rb=rb)
    w_mix = wts["w_out"]
    x3 = _ffn(x1, wts["ffn2_norm"], wts["ffn2_w_in"], wts["ffn2_w_out"],
              mix=(sb_o, hg_o, w_mix[:SB_WIDTH], w_mix[SB_WIDTH:]), tm=tm, fc=fc)
    return x3, kf, vf, state


def kernel(x_prompt, x_sample, cache_sb_k, cache_sb_v, state_hgrn, ffn1_norm, ffn1_w_in,
           ffn1_w_out, mix_norm, w_in, sb_q_gain, sb_k_gain, hg_lb_logits, sb_out_gain,
           hg_out_gain, w_out, ffn2_norm, ffn2_w_in, ffn2_w_out):
    depth = w_in.shape[0]
    bp, tp, d = x_prompt.shape
    bs, ts, _ = x_sample.shape
    past = cache_sb_k.shape[2]
    lb_all = jnp.cumsum(jax.nn.softmax(hg_lb_logits.astype(F32), axis=0), axis=0)
    hid = lax.broadcasted_iota(jnp.int32, (SB_WIDTH, SB_WIDTH), 0) // SB_HEAD_DIM
    hid_t = lax.broadcasted_iota(jnp.int32, (SB_WIDTH, SB_WIDTH), 1) // SB_HEAD_DIM
    head_mean = jnp.where(hid == hid_t, 1.0 / SB_HEAD_DIM, 0.0).astype(BF16)

    xp = x_prompt.reshape(bp * tp, d)
    xs = x_sample.reshape(bs * ts, d)
    outs = [[] for _ in range(6)]
    for l in range(depth):
        wts = dict(
            ffn1_norm=ffn1_norm[l], ffn1_w_in=ffn1_w_in[l].astype(BF16),
            ffn1_w_out=ffn1_w_out[l].astype(BF16), mix_norm=mix_norm[l],
            w_in=w_in[l].astype(BF16), sb_q_gain=sb_q_gain[l], sb_k_gain=sb_k_gain[l],
            sb_out_gain=sb_out_gain[l], hg_out_gain=hg_out_gain[l],
            w_out=w_out[l].astype(BF16), ffn2_norm=ffn2_norm[l],
            ffn2_w_in=ffn2_w_in[l].astype(BF16), ffn2_w_out=ffn2_w_out[l].astype(BF16),
            head_mean=head_mean)
        xp, kp, vp, sp = _layer(xp, wts, lb_all[l], batch=bp)
        cache = (cache_sb_k[l].reshape(bs * past, SB_WIDTH),
                 cache_sb_v[l].reshape(bs * past, SB_WIDTH))
        xs, kn, vn, sn = _layer(xs, wts, lb_all[l], batch=bs, cache=cache, s0=state_hgrn[l])
        hd = (SB_HEADS, SB_HEAD_DIM)
        for lst, val in zip(outs, (kp.reshape(bp, tp, *hd), vp.reshape(bp, tp, *hd), sp,
                                   kn.reshape(bs, ts, *hd), vn.reshape(bs, ts, *hd), sn)):
            lst.append(val)
    return (xp.reshape(bp, tp, d), xs.reshape(bs, ts, d)) + tuple(
        jnp.stack(lst, axis=0) for lst in outs)
```

```python
import functools

import jax
import jax.numpy as jnp
from jax import lax
from jax.experimental import pallas as pl
from jax.experimental.pallas import tpu as pltpu

EPS = 1e-6
SB_HEADS = 8
SB_HEAD_DIM = 64
SB_WIDTH = SB_HEADS * SB_HEAD_DIM
SB_SCALE = SB_HEAD_DIM ** -0.5
LOG2_E = 1.4426950408889634
EXP2_CLAMP = 126.0
HG_HEADS = 4
HG_DIM = 128
HG_WIDTH = HG_HEADS * HG_DIM
LANES = 128
SUB = 8
HEAD_PAIRS = SB_WIDTH // LANES

BF16 = jnp.bfloat16
F32 = jnp.float32

VMEM_LIMIT = 56 * 1024 * 1024
FFN_CHUNK = 256


def _dot(a, b):
    return jnp.dot(a, b, preferred_element_type=F32)


def _dot_nt(a, b):
    return lax.dot_general(a, b, (((1,), (1,)), ((), ())), preferred_element_type=F32)


def _dot_tn(a, b):
    return lax.dot_general(a, b, (((0,), (0,)), ((), ())), preferred_element_type=F32)


def _sigmoid(x):
    return 1.0 / (1.0 + jnp.exp(-x))


def _rms_rows(x, g):
    ms = jnp.mean(x * x, axis=-1, keepdims=True)
    return x * lax.rsqrt(ms + EPS) * g


def _params(n_grid):
    return pltpu.CompilerParams(
        dimension_semantics=("arbitrary",) * n_grid, vmem_limit_bytes=VMEM_LIMIT)


def _ffn_kernel(*refs, with_mix):
    if with_mix:
        (x_ref, ma_ref, mb_ref, woa_ref, wob_ref, g_ref, wa_ref, wb_ref, wo_ref,
         o_ref, h_ref) = refs
    else:
        x_ref, g_ref, wa_ref, wb_ref, wo_ref, o_ref, h_ref = refs

    @pl.when(pl.program_id(1) == 0)
    def _():
        x = x_ref[...]
        if with_mix:
            x = x + _dot(ma_ref[...], woa_ref[...]) + _dot(mb_ref[...], wob_ref[...])
        o_ref[...] = x
        h_ref[...] = _rms_rows(x, g_ref[...]).astype(BF16)

    h = h_ref[...]
    a = _dot(h, wa_ref[0])
    b = _dot(h, wb_ref[0])
    act = a * (0.5 * _sigmoid(a)) * b
    o_ref[...] += _dot(act.astype(BF16), wo_ref[...])


def _ffn(x, norm_g, w_in, w_out, mix=None, *, tm, fc):
    rows, d = x.shape
    nj = w_out.shape[0] // fc
    row_spec = pl.BlockSpec((tm, d), lambda i, j: (i, 0))
    in_specs = [row_spec]
    args = [x]
    if mix is not None:
        ma, mb, woa, wob = mix
        in_specs += [pl.BlockSpec((tm, ma.shape[1]), lambda i, j: (i, 0)),
                     pl.BlockSpec((tm, mb.shape[1]), lambda i, j: (i, 0)),
                     pl.BlockSpec(woa.shape, lambda i, j: (0, 0)),
                     pl.BlockSpec(wob.shape, lambda i, j: (0, 0))]
        args += [ma, mb, woa, wob]
    in_specs += [pl.BlockSpec((1, d), lambda i, j: (0, 0)),
                 pl.BlockSpec((1, d, fc), lambda i, j: (j, 0, 0)),
                 pl.BlockSpec((1, d, fc), lambda i, j: (j + nj, 0, 0)),
                 pl.BlockSpec((fc, d), lambda i, j: (j, 0))]
    args += [norm_g.reshape(1, d), w_in, w_in, w_out]
    return pl.pallas_call(
        functools.partial(_ffn_kernel, with_mix=mix is not None),
        grid=(rows // tm, nj),
        in_specs=in_specs,
        out_specs=row_spec,
        out_shape=jax.ShapeDtypeStruct((rows, d), F32),
        scratch_shapes=[pltpu.VMEM((tm, d), BF16)],
        compiler_params=_params(2),
        name="ffn_mix" if mix is not None else "ffn",
    )(*args)


def _proj_kernel(x_ref, g_ref, w_ref, qg_ref, kg_ref, lb_ref, hm_ref,
                 kf_ref, vf_ref, qb_ref, kb_ref, vb_ref,
                 qh_ref, kin_ref, lf_ref, hi_ref, gt_ref):
    h = _rms_rows(x_ref[...], g_ref[...]).astype(BF16)
    w = SB_WIDTH

    def col(c):
        return _dot(h, w_ref[:, c * w:(c + 1) * w])

    def head_norm(a, gain):
        ms = _dot((a * a).astype(BF16), hm_ref[...])
        return a * lax.rsqrt(ms + EPS) * gain

    qn = head_norm(col(0), qg_ref[...])
    qb_ref[...] = (qn * (SB_SCALE * LOG2_E)).astype(BF16)
    kn = head_norm(col(1), kg_ref[...])
    kf_ref[...] = kn
    kb_ref[...] = kn.astype(BF16)
    sv = col(2)
    vf_ref[...] = sv
    vb_ref[...] = sv.astype(BF16)

    hq = col(3)
    qh_ref[...] = hq * _sigmoid(hq)
    hf = col(4)
    lb = lb_ref[...]
    f = lb + (1.0 - lb) * _sigmoid(hf)
    lf_ref[...] = jnp.log(f)
    kin_ref[...] = (1.0 - lb) * _sigmoid(-hf)
    hi_ref[...] = col(5).astype(BF16)
    hg = col(6)
    gt_ref[...] = hg * _sigmoid(hg)


def _proj(x, norm_g, w_in, q_gain, k_gain, lb, head_mean, *, tm):
    rows, d = x.shape
    w = SB_WIDTH
    row_spec = pl.BlockSpec((tm, w), lambda i: (i, 0))
    vec = lambda n: pl.BlockSpec((1, n), lambda i: (0, 0))
    f32o = jax.ShapeDtypeStruct((rows, w), F32)
    bf16o = jax.ShapeDtypeStruct((rows, w), BF16)
    return pl.pallas_call(
        _proj_kernel,
        grid=(rows // tm,),
        in_specs=[pl.BlockSpec((tm, d), lambda i: (i, 0)), vec(d),
                  pl.BlockSpec(w_in.shape, lambda i: (0, 0)),
                  vec(w), vec(w), vec(w),
                  pl.BlockSpec(head_mean.shape, lambda i: (0, 0))],
        out_specs=[row_spec] * 10,
        out_shape=[f32o, f32o, bf16o, bf16o, bf16o, f32o, f32o, f32o, bf16o, f32o],
        compiler_params=_params(1),
        name="proj",
    )(x, norm_g.reshape(1, d), w_in,
      jnp.tile(q_gain, SB_HEADS).reshape(1, w), jnp.tile(k_gain, SB_HEADS).reshape(1, w),
      lb.reshape(1, w), head_mean)


def _sb_tiles(ops, upper, mask):
    zs = [_dot_nt(q, kblk) for q, kblk, _ in ops]
    suffixes = []
    for z in zs:
        drop = jnp.maximum(z, jnp.log2(1.0 + jnp.exp2(jnp.minimum(z, EXP2_CLAMP))))
        if mask is not None:
            drop = jnp.where(mask, drop, 0.0)
        suffixes.append(_dot(drop.astype(BF16), upper))
    out = []
    for z, suffix, (_, _, vblk) in zip(zs, suffixes, ops):
        w = jnp.exp2(z - suffix)
        if mask is not None:
            w = jnp.where(mask, w, 0.0)
        out.append((_dot(w.astype(BF16), vblk), suffix[:, :1]))
    return out


def _sb_finish(acc0, acc1, gain):
    lane = lax.broadcasted_iota(jnp.int32, (1, LANES), 1)
    first = lane < SB_HEAD_DIM
    o = jnp.where(first, acc0, acc1)
    sq = o * o
    s0 = jnp.sum(jnp.where(first, sq, 0.0), axis=-1, keepdims=True)
    s1 = jnp.sum(jnp.where(first, 0.0, sq), axis=-1, keepdims=True)
    ms = jnp.where(first, s0, s1) * (1.0 / SB_HEAD_DIM)
    return o * lax.rsqrt(ms + EPS) * gain


def _head_masked(q):
    lane = lax.broadcasted_iota(jnp.int32, (1, LANES), 1)
    first = lane < SB_HEAD_DIM
    zero = jnp.zeros_like(q)
    return jnp.where(first, q, zero), jnp.where(first, zero, q)


def _sb_prompt_kernel(q_ref, k_ref, v_ref, gain_ref, up_ref, o_ref, acc_ref, car_ref, *, tq):
    i = pl.program_id(2)
    qms = _head_masked(q_ref[...])
    upper = up_ref[...]
    row = lax.broadcasted_iota(jnp.int32, (tq, tq), 0)
    colid = lax.broadcasted_iota(jnp.int32, (tq, tq), 1)
    causal = colid < row

    def blocks(kbs, mask):
        ops = []
        for kb in kbs:
            start = pl.multiple_of(kb * tq, tq)
            kblk = k_ref[pl.ds(start, tq), :]
            vblk = v_ref[pl.ds(start, tq), :]
            ops += [(qms[0], kblk, vblk), (qms[1], kblk, vblk)]
        res = _sb_tiles(ops, upper, mask)
        for h in range(2):
            carry = car_ref[h]
            acc = acc_ref[h]
            for pv, total in res[h::2]:
                acc = acc + jnp.exp2(-carry) * pv
                carry = carry + total
            car_ref[h] = carry
            acc_ref[h] = acc

    acc_ref[...] = jnp.zeros_like(acc_ref)
    car_ref[...] = jnp.zeros_like(car_ref)
    blocks([i], causal)

    @pl.when(i % 2 == 1)
    def _():
        blocks([i - 1], None)

    n_pairs = i // 2

    def pair(j, carry):
        top = 2 * (n_pairs - j) - 1
        blocks([top, top - 1], None)
        return carry

    lax.fori_loop(0, n_pairs, pair, 0)
    o_ref[...] = _sb_finish(acc_ref[0], acc_ref[1], gain_ref[...]).astype(o_ref.dtype)


def _sb_prompt(q, k, v, gain_pair, upper, *, batch, tq):
    rows, w = q.shape
    t = rows // batch
    nq = t // tq
    qspec = pl.BlockSpec((tq, LANES), lambda b, p, i: (b * nq + i, p))
    kvspec = pl.BlockSpec((t, LANES), lambda b, p, i: (b, p))
    return pl.pallas_call(
        functools.partial(_sb_prompt_kernel, tq=tq),
        grid=(batch, HEAD_PAIRS, nq),
        in_specs=[qspec, kvspec, kvspec,
                  pl.BlockSpec((1, LANES), lambda b, p, i: (0, 0)),
                  pl.BlockSpec((tq, tq), lambda b, p, i: (0, 0))],
        out_specs=qspec,
        out_shape=jax.ShapeDtypeStruct((rows, w), BF16),
        scratch_shapes=[pltpu.VMEM((2, tq, LANES), F32), pltpu.VMEM((2, tq, 1), F32)],
        compiler_params=_params(3),
        name="sb_prompt",
    )(q, k, v, gain_pair, upper)


def _sb_sample_kernel(q_ref, kn_ref, vn_ref, ck_ref, cv_ref, gain_ref, up_ref, upn_ref,
                      o_ref, *, tk, n_blocks, t_new):
    qms = _head_masked(q_ref[...])
    upper = up_ref[...]
    row = lax.broadcasted_iota(jnp.int32, (t_new, t_new), 0)
    colid = lax.broadcasted_iota(jnp.int32, (t_new, t_new), 1)
    causal = colid < row
    kn = kn_ref[...]
    vn = vn_ref[...]

    res = _sb_tiles([(qm, kn, vn) for qm in qms], upn_ref[...], causal)
    accs = [r[0] for r in res]
    carries = [r[1] for r in res]
    ops = []
    for kb in range(n_blocks - 1, -1, -1):
        rows = slice(kb * tk, (kb + 1) * tk)
        kblk = ck_ref[rows, :].astype(BF16)
        vblk = cv_ref[rows, :].astype(BF16)
        ops += [(qm, kblk, vblk) for qm in qms]
    for n, (pv, total) in enumerate(_sb_tiles(ops, upper, None)):
        h = n % 2
        accs[h] = accs[h] + jnp.exp2(-carries[h]) * pv
        carries[h] = carries[h] + total
    o_ref[...] = _sb_finish(accs[0], accs[1], gain_ref[...]).astype(o_ref.dtype)


def _sb_sample(q, k_new, v_new, cache_k, cache_v, gain_pair, upper, upper_new, *, batch, tk):
    rows, w = q.shape
    t_new = rows // batch
    past = cache_k.shape[0] // batch
    qspec = pl.BlockSpec((t_new, LANES), lambda b, p: (b, p))
    cspec = pl.BlockSpec((past, LANES), lambda b, p: (b, p))
    return pl.pallas_call(
        functools.partial(_sb_sample_kernel, tk=tk, n_blocks=past // tk, t_new=t_new),
        grid=(batch, HEAD_PAIRS),
        in_specs=[qspec, qspec, qspec, cspec, cspec,
                  pl.BlockSpec((1, LANES), lambda b, p: (0, 0)),
                  pl.BlockSpec((tk, tk), lambda b, p: (0, 0)),
                  pl.BlockSpec((t_new, t_new), lambda b, p: (0, 0))],
        out_specs=qspec,
        out_shape=jax.ShapeDtypeStruct((rows, w), BF16),
        compiler_params=_params(2),
        name="sb_sample",
    )(q, k_new, v_new, cache_k, cache_v, gain_pair, upper, upper_new)


def _hgrn_chunk(q, k, g, v, st, lower, ones, c):
    n = c // SUB
    g1 = g.astype(BF16)
    r1 = g - g1.astype(F32)
    g2 = r1.astype(BF16)
    g3 = (r1 - g2.astype(F32)).astype(BF16)
    bs = _dot(lower, jnp.concatenate([g1, g2, g3], axis=1))
    b = bs[:, :HG_DIM] + (bs[:, HG_DIM:2 * HG_DIM] + bs[:, 2 * HG_DIM:])
    b_last = b[c - 1:c, :]

    b3 = b.reshape(n, SUB, HG_DIM)
    q3 = q.reshape(n, SUB, HG_DIM)
    k3 = k.reshape(n, SUB, HG_DIM)
    kt = (k3 * jnp.exp(b3[:, SUB - 1:SUB, :] - b3)).reshape(c, HG_DIM).astype(BF16)

    row = lax.broadcasted_iota(jnp.int32, (c, 1), 0)
    a = None
    if n > 1:
        qs, ks = [], []
        zero = jnp.zeros_like(kt)
        for j in range(n - 1):
            ref = b[SUB * j + SUB - 1:SUB * j + SUB, :]
            qj = q * jnp.exp(jnp.minimum(b - ref, 0.0))
            qs.append(jnp.where(row >= SUB * (j + 1), qj, 0.0).astype(BF16))
            ks.append(jnp.where((row >= SUB * j) & (row < SUB * (j + 1)), kt, zero))
        a = _dot_nt(jnp.concatenate(qs, axis=1), jnp.concatenate(ks, axis=1))

    sub = lax.broadcasted_iota(jnp.int32, (1, SUB, 1), 1)
    ps = []
    for s in range(SUB):
        e = jnp.exp(jnp.minimum(b3 - b3[:, s:s + 1, :], 0.0))
        p = q3 * e * k3[:, s:s + 1, :]
        ps.append(jnp.where(sub >= s, p, 0.0).reshape(c, HG_DIM).astype(BF16))
    sums = _dot(jnp.concatenate(ps, axis=0), ones)
    colid = lax.broadcasted_iota(jnp.int32, (c, c), 1)
    base = (lax.broadcasted_iota(jnp.int32, (c, c), 0) // SUB) * SUB
    for s in range(SUB):
        d = jnp.where(colid == base + s, sums[s * c:(s + 1) * c, :], 0.0)
        a = d if a is None else a + d

    o = _dot_nt((q * jnp.exp(b)).astype(BF16), st.astype(BF16)) + _dot(a.astype(BF16), v)
    khat = (k * jnp.exp(b_last - b)).astype(BF16)
    st_new = st * jnp.exp(b_last) + _dot_tn(v, khat)
    return o, st_new


def _hgrn_kernel(*refs, c, n_chunks, has_s0):
    if has_s0:
        (qh_ref, kin_ref, lf_ref, hi_ref, gt_ref, gain_ref, low_ref, ones_ref, s0_ref,
         o_ref, s_ref, st_ref) = refs
    else:
        (qh_ref, kin_ref, lf_ref, hi_ref, gt_ref, gain_ref, low_ref, ones_ref,
         o_ref, s_ref, st_ref) = refs
    step = pl.program_id(1)

    @pl.when(step == 0)
    def _():
        for h in range(HG_HEADS):
            st_ref[h] = s0_ref[0, h].T if has_s0 else jnp.zeros((HG_DIM, HG_DIM), F32)

    lower = low_ref[...]
    ones = ones_ref[...]
    gain = gain_ref[...]

    def chunk(ci, carry):
        r0 = pl.multiple_of(ci * c, c)
        for h in range(HG_HEADS):
            ln = slice(h * HG_DIM, (h + 1) * HG_DIM)
            o, st = _hgrn_chunk(qh_ref[pl.ds(r0, c), ln], kin_ref[pl.ds(r0, c), ln],
                                lf_ref[pl.ds(r0, c), ln], hi_ref[pl.ds(r0, c), ln],
                                st_ref[h], lower, ones, c)
            st_ref[h] = st
            y = _rms_rows(o, gain) * gt_ref[pl.ds(r0, c), ln]
            o_ref[pl.ds(r0, c), ln] = y.astype(o_ref.dtype)
        return carry

    lax.fori_loop(0, n_chunks, chunk, 0)

    @pl.when(step == pl.num_programs(1) - 1)
    def _():
        for h in range(HG_HEADS):
            s_ref[0, h] = st_ref[h].T


def _hgrn(qh, kin, lf, hi, gate, gain, s0, *, batch, c, rb):
    rows, w = qh.shape
    t = rows // batch
    nsteps = t // rb
    row_spec = pl.BlockSpec((rb, w), lambda b, s: (b * nsteps + s, 0))
    lower = jnp.tril(jnp.ones((c, c), BF16))
    ones = jnp.ones((HG_DIM, c), BF16)
    state_spec = pl.BlockSpec((1, HG_HEADS, HG_DIM, HG_DIM), lambda b, s: (b, 0, 0, 0))
    in_specs = [row_spec] * 5 + [pl.BlockSpec((1, HG_DIM), lambda b, s: (0, 0)),
                                 pl.BlockSpec((c, c), lambda b, s: (0, 0)),
                                 pl.BlockSpec((HG_DIM, c), lambda b, s: (0, 0))]
    args = [qh, kin, lf, hi, gate, gain.reshape(1, HG_DIM), lower, ones]
    if s0 is not None:
        in_specs.append(state_spec)
        args.append(s0)
    return pl.pallas_call(
        functools.partial(_hgrn_kernel, c=c, n_chunks=rb // c, has_s0=s0 is not None),
        grid=(batch, nsteps),
        in_specs=in_specs,
        out_specs=[row_spec, state_spec],
        out_shape=[jax.ShapeDtypeStruct((rows, w), BF16),
                   jax.ShapeDtypeStruct((batch, HG_HEADS, HG_DIM, HG_DIM), F32)],
        scratch_shapes=[pltpu.VMEM((HG_HEADS, HG_DIM, HG_DIM), F32)],
        compiler_params=_params(2),
        name="hgrn_s0" if s0 is not None else "hgrn",
    )(*args)


def _tile_sizes(rows, t):
    tm = min(1024, rows)
    tq = min(256, t)
    c = min(64, t)
    rb = min(512, t)
    return tm, tq, c, rb


def _column_chunks(w, fc):
    d, n = w.shape
    return w.reshape(d, n // fc, fc).transpose(1, 0, 2)


def _suffix_ones(n):
    r = lax.broadcasted_iota(jnp.int32, (n, n), 0)
    col = lax.broadcasted_iota(jnp.int32, (n, n), 1)
    return (r >= col).astype(BF16)


def _layer(x, wts, lb, *, batch, cache=None, s0=None):
    rows, _ = x.shape
    t = rows // batch
    tm, tq, c, rb = _tile_sizes(rows, t)
    fc = FFN_CHUNK
    x1 = _ffn(x, wts["ffn1_norm"], wts["ffn1_w_in"], wts["ffn1_w_out"], tm=tm, fc=fc)
    (kf, vf, qb, kb, vb, qh, kin, lf, hi, gate) = _proj(
        x1, wts["mix_norm"], wts["w_in"], wts["sb_q_gain"], wts["sb_k_gain"], lb,
        wts["head_mean"], tm=min(512, rows))
    gain_pair = jnp.tile(wts["sb_out_gain"], LANES // SB_HEAD_DIM).reshape(1, LANES)
    if cache is None:
        sb_o = _sb_prompt(qb, kb, vb, gain_pair, _suffix_ones(tq), batch=batch, tq=tq)
    else:
        tk = min(256, cache[0].shape[0] // batch)
        sb_o = _sb_sample(qb, kb, vb, cache[0], cache[1], gain_pair, _suffix_ones(tk),
                          _suffix_ones(t), batch=batch, tk=tk)
    hg_o, state = _hgrn(qh, kin, lf, hi, gate, wts["hg_out_gain"], s0, batch=batch, c=c, rb=rb)
    w_mix = wts["w_out"]
    x3 = _ffn(x1, wts["ffn2_norm"], wts["ffn2_w_in"], wts["ffn2_w_out"],
              mix=(sb_o, hg_o, w_mix[:SB_WIDTH], w_mix[SB_WIDTH:]), tm=tm, fc=fc)
    return x3, kf, vf, state


def kernel(x_prompt, x_sample, cache_sb_k, cache_sb_v, state_hgrn, ffn1_norm, ffn1_w_in,
           ffn1_w_out, mix_norm, w_in, sb_q_gain, sb_k_gain, hg_lb_logits, sb_out_gain,
           hg_out_gain, w_out, ffn2_norm, ffn2_w_in, ffn2_w_out):
    depth = w_in.shape[0]
    bp, tp, d = x_prompt.shape
    bs, ts, _ = x_sample.shape
    past = cache_sb_k.shape[2]
    lb_all = jnp.cumsum(jax.nn.softmax(hg_lb_logits.astype(F32), axis=0), axis=0)
    hid = lax.broadcasted_iota(jnp.int32, (SB_WIDTH, SB_WIDTH), 0) // SB_HEAD_DIM
    hid_t = lax.broadcasted_iota(jnp.int32, (SB_WIDTH, SB_WIDTH), 1) // SB_HEAD_DIM
    head_mean = jnp.where(hid == hid_t, 1.0 / SB_HEAD_DIM, 0.0).astype(BF16)

    xp = x_prompt.reshape(bp * tp, d)
    xs = x_sample.reshape(bs * ts, d)
    outs = [[] for _ in range(6)]
    for l in range(depth):
        wts = dict(
            ffn1_norm=ffn1_norm[l],
            ffn1_w_in=_column_chunks(ffn1_w_in[l].astype(BF16), FFN_CHUNK),
            ffn1_w_out=ffn1_w_out[l].astype(BF16), mix_norm=mix_norm[l],
            w_in=w_in[l].astype(BF16), sb_q_gain=sb_q_gain[l], sb_k_gain=sb_k_gain[l],
            sb_out_gain=sb_out_gain[l], hg_out_gain=hg_out_gain[l],
            w_out=w_out[l].astype(BF16), ffn2_norm=ffn2_norm[l],
            ffn2_w_in=_column_chunks(ffn2_w_in[l].astype(BF16), FFN_CHUNK),
            ffn2_w_out=ffn2_w_out[l].astype(BF16),
            head_mean=head_mean)
        xp, kp, vp, sp = _layer(xp, wts, lb_all[l], batch=bp)
        cache = (cache_sb_k[l].reshape(bs * past, SB_WIDTH),
                 cache_sb_v[l].reshape(bs * past, SB_WIDTH))
        xs, kn, vn, sn = _layer(xs, wts, lb_all[l], batch=bs, cache=cache, s0=state_hgrn[l])
        hd = (SB_HEADS, SB_HEAD_DIM)
        for lst, val in zip(outs, (kp.reshape(bp, tp, *hd), vp.reshape(bp, tp, *hd), sp,
                                   kn.reshape(bs, ts, *hd), vn.reshape(bs, ts, *hd), sn)):
            lst.append(val)
    return (xp.reshape(bp, tp, d), xs.reshape(bs, ts, d)) + tuple(
        jnp.stack(lst, axis=0) for lst in outs)
```

```python
import functools

import jax
import jax.numpy as jnp
from jax import lax
from jax.experimental import pallas as pl
from jax.experimental.pallas import tpu as pltpu

EPS = 1e-6
SB_HEADS = 8
SB_HEAD_DIM = 64
SB_WIDTH = SB_HEADS * SB_HEAD_DIM
SB_SCALE = SB_HEAD_DIM ** -0.5
LOG2_E = 1.4426950408889634
EXP2_CLAMP = 126.0
HG_HEADS = 4
HG_DIM = 128
HG_WIDTH = HG_HEADS * HG_DIM
LANES = 128
SUB = 8
HEAD_PAIRS = SB_WIDTH // LANES

BF16 = jnp.bfloat16
F32 = jnp.float32

VMEM_LIMIT = 56 * 1024 * 1024
FFN_CHUNK = 256


def _dot(a, b):
    return jnp.dot(a, b, preferred_element_type=F32)


def _dot_nt(a, b):
    return lax.dot_general(a, b, (((1,), (1,)), ((), ())), preferred_element_type=F32)


def _dot_tn(a, b):
    return lax.dot_general(a, b, (((0,), (0,)), ((), ())), preferred_element_type=F32)


def _sigmoid(x):
    return 1.0 / (1.0 + jnp.exp(-x))


def _rms_rows(x, g):
    ms = jnp.mean(x * x, axis=-1, keepdims=True)
    return x * lax.rsqrt(ms + EPS) * g


def _params(n_grid):
    return pltpu.CompilerParams(
        dimension_semantics=("arbitrary",) * n_grid, vmem_limit_bytes=VMEM_LIMIT)


def _ffn_kernel(*refs, with_mix):
    if with_mix:
        (x_ref, ma_ref, mb_ref, woa_ref, wob_ref, g_ref, wa_ref, wb_ref, wo_ref,
         o_ref, h_ref) = refs
    else:
        x_ref, g_ref, wa_ref, wb_ref, wo_ref, o_ref, h_ref = refs

    @pl.when(pl.program_id(1) == 0)
    def _():
        x = x_ref[...]
        if with_mix:
            x = x + _dot(ma_ref[...], woa_ref[...]) + _dot(mb_ref[...], wob_ref[...])
        o_ref[...] = x
        h_ref[...] = _rms_rows(x, g_ref[...]).astype(BF16)

    h = h_ref[...]
    a = _dot(h, wa_ref[0])
    b = _dot(h, wb_ref[0])
    act = a * (0.5 * _sigmoid(a)) * b
    o_ref[...] += _dot(act.astype(BF16), wo_ref[...])


def _ffn(x, norm_g, w_in, w_out, mix=None, *, tm, fc):
    rows, d = x.shape
    nj = w_out.shape[0] // fc
    row_spec = pl.BlockSpec((tm, d), lambda i, j: (i, 0))
    in_specs = [row_spec]
    args = [x]
    if mix is not None:
        ma, mb, woa, wob = mix
        in_specs += [pl.BlockSpec((tm, ma.shape[1]), lambda i, j: (i, 0)),
                     pl.BlockSpec((tm, mb.shape[1]), lambda i, j: (i, 0)),
                     pl.BlockSpec(woa.shape, lambda i, j: (0, 0)),
                     pl.BlockSpec(wob.shape, lambda i, j: (0, 0))]
        args += [ma, mb, woa, wob]
    in_specs += [pl.BlockSpec((1, d), lambda i, j: (0, 0)),
                 pl.BlockSpec((1, d, fc), lambda i, j: (j, 0, 0)),
                 pl.BlockSpec((1, d, fc), lambda i, j: (j + nj, 0, 0)),
                 pl.BlockSpec((fc, d), lambda i, j: (j, 0))]
    args += [norm_g.reshape(1, d), w_in, w_in, w_out]
    return pl.pallas_call(
        functools.partial(_ffn_kernel, with_mix=mix is not None),
        grid=(rows // tm, nj),
        in_specs=in_specs,
        out_specs=row_spec,
        out_shape=jax.ShapeDtypeStruct((rows, d), F32),
        scratch_shapes=[pltpu.VMEM((tm, d), BF16)],
        compiler_params=_params(2),
        name="ffn_mix" if mix is not None else "ffn",
    )(*args)


def _proj_kernel(x_ref, g_ref, w_ref, qg_ref, kg_ref, lb_ref, hm_ref,
                 kf_ref, vf_ref, qb_ref, kb_ref, vb_ref,
                 qh_ref, kin_ref, lf_ref, hi_ref, gt_ref, *, feature_major):
    h = _rms_rows(x_ref[...], g_ref[...]).astype(BF16)
    w = SB_WIDTH

    def col(c):
        return _dot(h, w_ref[:, c * w:(c + 1) * w])

    def head_norm(a, gain):
        ms = _dot((a * a).astype(BF16), hm_ref[...])
        return a * lax.rsqrt(ms + EPS) * gain

    qn = head_norm(col(0), qg_ref[...])
    qb_ref[...] = (qn * (SB_SCALE * LOG2_E)).astype(BF16)
    kn = head_norm(col(1), kg_ref[...])
    sv = col(2)
    if feature_major:
        kf_ref[0] = kn.T
        vf_ref[0] = sv.T
    else:
        kf_ref[...] = kn
        vf_ref[...] = sv
    kb_ref[...] = kn.astype(BF16)
    vb_ref[...] = sv.astype(BF16)

    hq = col(3)
    qh_ref[...] = hq * _sigmoid(hq)
    hf = col(4)
    lb = lb_ref[...]
    f = lb + (1.0 - lb) * _sigmoid(hf)
    lf_ref[...] = jnp.log(f)
    kin_ref[...] = (1.0 - lb) * _sigmoid(-hf)
    hi_ref[...] = col(5).astype(BF16)
    hg = col(6)
    gt_ref[...] = hg * _sigmoid(hg)


def _proj(x, norm_g, w_in, q_gain, k_gain, lb, head_mean, *, tm, batch):
    rows, d = x.shape
    t = rows // batch
    w = SB_WIDTH
    feature_major = t % tm == 0
    row_spec = pl.BlockSpec((tm, w), lambda i: (i, 0))
    vec = lambda n: pl.BlockSpec((1, n), lambda i: (0, 0))
    f32o = jax.ShapeDtypeStruct((rows, w), F32)
    bf16o = jax.ShapeDtypeStruct((rows, w), BF16)
    if feature_major:
        nt = t // tm
        kv_spec = pl.BlockSpec((1, w, tm), lambda i: (i // nt, 0, i % nt))
        kv_shape = jax.ShapeDtypeStruct((batch, w, t), F32)
    else:
        kv_spec, kv_shape = row_spec, f32o
    return pl.pallas_call(
        functools.partial(_proj_kernel, feature_major=feature_major),
        grid=(rows // tm,),
        in_specs=[pl.BlockSpec((tm, d), lambda i: (i, 0)), vec(d),
                  pl.BlockSpec(w_in.shape, lambda i: (0, 0)),
                  vec(w), vec(w), vec(w),
                  pl.BlockSpec(head_mean.shape, lambda i: (0, 0))],
        out_specs=[kv_spec, kv_spec] + [row_spec] * 8,
        out_shape=[kv_shape, kv_shape, bf16o, bf16o, bf16o, f32o, f32o, f32o, bf16o, f32o],
        compiler_params=_params(1),
        name="proj",
    )(x, norm_g.reshape(1, d), w_in,
      jnp.tile(q_gain, SB_HEADS).reshape(1, w), jnp.tile(k_gain, SB_HEADS).reshape(1, w),
      lb.reshape(1, w), head_mean)


def _sb_tiles(ops, upper, mask, feature_major=False):
    scores, weighted = (_dot, _dot_nt) if feature_major else (_dot_nt, _dot)
    zs = [scores(q, kblk) for q, kblk, _ in ops]
    suffixes = []
    for z in zs:
        drop = jnp.maximum(z, jnp.log2(1.0 + jnp.exp2(jnp.minimum(z, EXP2_CLAMP))))
        if mask is not None:
            drop = jnp.where(mask, drop, 0.0)
        suffixes.append(_dot(drop.astype(BF16), upper))
    out = []
    for z, suffix, (_, _, vblk) in zip(zs, suffixes, ops):
        w = jnp.exp2(z - suffix)
        if mask is not None:
            w = jnp.where(mask, w, 0.0)
        out.append((weighted(w.astype(BF16), vblk), suffix[:, :1]))
    return out


def _sb_finish(acc0, acc1, gain):
    lane = lax.broadcasted_iota(jnp.int32, (1, LANES), 1)
    first = lane < SB_HEAD_DIM
    o = jnp.where(first, acc0, acc1)
    sq = o * o
    s0 = jnp.sum(jnp.where(first, sq, 0.0), axis=-1, keepdims=True)
    s1 = jnp.sum(jnp.where(first, 0.0, sq), axis=-1, keepdims=True)
    ms = jnp.where(first, s0, s1) * (1.0 / SB_HEAD_DIM)
    return o * lax.rsqrt(ms + EPS) * gain


def _head_masked(q):
    lane = lax.broadcasted_iota(jnp.int32, (1, LANES), 1)
    first = lane < SB_HEAD_DIM
    zero = jnp.zeros_like(q)
    return jnp.where(first, q, zero), jnp.where(first, zero, q)


def _sb_prompt_kernel(q_ref, k_ref, v_ref, gain_ref, up_ref, o_ref, acc_ref, car_ref, *, tq):
    i = pl.program_id(2)
    qms = _head_masked(q_ref[...])
    upper = up_ref[...]
    row = lax.broadcasted_iota(jnp.int32, (tq, tq), 0)
    colid = lax.broadcasted_iota(jnp.int32, (tq, tq), 1)
    causal = colid < row

    def blocks(kbs, mask):
        ops = []
        for kb in kbs:
            start = pl.multiple_of(kb * tq, tq)
            kblk = k_ref[pl.ds(start, tq), :]
            vblk = v_ref[pl.ds(start, tq), :]
            ops += [(qms[0], kblk, vblk), (qms[1], kblk, vblk)]
        res = _sb_tiles(ops, upper, mask)
        for h in range(2):
            carry = car_ref[h]
            acc = acc_ref[h]
            for pv, total in res[h::2]:
                acc = acc + jnp.exp2(-carry) * pv
                carry = carry + total
            car_ref[h] = carry
            acc_ref[h] = acc

    acc_ref[...] = jnp.zeros_like(acc_ref)
    car_ref[...] = jnp.zeros_like(car_ref)
    blocks([i], causal)

    @pl.when(i % 2 == 1)
    def _():
        blocks([i - 1], None)

    n_pairs = i // 2

    def pair(j, carry):
        top = 2 * (n_pairs - j) - 1
        blocks([top, top - 1], None)
        return carry

    lax.fori_loop(0, n_pairs, pair, 0)
    o_ref[...] = _sb_finish(acc_ref[0], acc_ref[1], gain_ref[...]).astype(o_ref.dtype)


def _sb_prompt(q, k, v, gain_pair, upper, *, batch, tq):
    rows, w = q.shape
    t = rows // batch
    nq = t // tq
    qspec = pl.BlockSpec((tq, LANES), lambda b, p, i: (b * nq + i, p))
    kvspec = pl.BlockSpec((t, LANES), lambda b, p, i: (b, p))
    return pl.pallas_call(
        functools.partial(_sb_prompt_kernel, tq=tq),
        grid=(batch, HEAD_PAIRS, nq),
        in_specs=[qspec, kvspec, kvspec,
                  pl.BlockSpec((1, LANES), lambda b, p, i: (0, 0)),
                  pl.BlockSpec((tq, tq), lambda b, p, i: (0, 0))],
        out_specs=qspec,
        out_shape=jax.ShapeDtypeStruct((rows, w), BF16),
        scratch_shapes=[pltpu.VMEM((2, tq, LANES), F32), pltpu.VMEM((2, tq, 1), F32)],
        compiler_params=_params(3),
        name="sb_prompt",
    )(q, k, v, gain_pair, upper)


def _sb_sample_kernel(q_ref, kn_ref, vn_ref, ck_ref, cv_ref, gain_ref, up_ref, upn_ref,
                      o_ref, *, tk, n_blocks, t_new):
    qms = _head_masked(q_ref[...])
    upper = up_ref[...]
    row = lax.broadcasted_iota(jnp.int32, (t_new, t_new), 0)
    colid = lax.broadcasted_iota(jnp.int32, (t_new, t_new), 1)
    causal = colid < row
    kn = kn_ref[...]
    vn = vn_ref[...]

    res = _sb_tiles([(qm, kn, vn) for qm in qms], upn_ref[...], causal)
    accs = [r[0] for r in res]
    carries = [r[1] for r in res]
    ops = []
    for kb in range(n_blocks - 1, -1, -1):
        keys = slice(kb * tk, (kb + 1) * tk)
        kblk = ck_ref[:, keys].astype(BF16)
        vblk = cv_ref[:, keys].astype(BF16)
        ops += [(qm, kblk, vblk) for qm in qms]
    for n, (pv, total) in enumerate(_sb_tiles(ops, upper, None, feature_major=True)):
        h = n % 2
        accs[h] = accs[h] + jnp.exp2(-carries[h]) * pv
        carries[h] = carries[h] + total
    o_ref[...] = _sb_finish(accs[0], accs[1], gain_ref[...]).astype(o_ref.dtype)


def _sb_sample(q, k_new, v_new, cache_k, cache_v, gain_pair, upper, upper_new, *, batch, tk):
    rows, w = q.shape
    t_new = rows // batch
    past = cache_k.shape[1]
    qspec = pl.BlockSpec((t_new, LANES), lambda b, p: (b, p))
    cspec = pl.BlockSpec((LANES, past), lambda b, p: (b * HEAD_PAIRS + p, 0))
    return pl.pallas_call(
        functools.partial(_sb_sample_kernel, tk=tk, n_blocks=past // tk, t_new=t_new),
        grid=(batch, HEAD_PAIRS),
        in_specs=[qspec, qspec, qspec, cspec, cspec,
                  pl.BlockSpec((1, LANES), lambda b, p: (0, 0)),
                  pl.BlockSpec((tk, tk), lambda b, p: (0, 0)),
                  pl.BlockSpec((t_new, t_new), lambda b, p: (0, 0))],
        out_specs=qspec,
        out_shape=jax.ShapeDtypeStruct((rows, w), BF16),
        compiler_params=_params(2),
        name="sb_sample",
    )(q, k_new, v_new, cache_k, cache_v, gain_pair, upper, upper_new)


def _hgrn_chunk(q, k, g, v, st, lower, ones, c):
    n = c // SUB
    g1 = g.astype(BF16)
    r1 = g - g1.astype(F32)
    g2 = r1.astype(BF16)
    g3 = (r1 - g2.astype(F32)).astype(BF16)
    bs = _dot(lower, jnp.concatenate([g1, g2, g3], axis=1))
    b = bs[:, :HG_DIM] + (bs[:, HG_DIM:2 * HG_DIM] + bs[:, 2 * HG_DIM:])
    b_last = b[c - 1:c, :]

    b3 = b.reshape(n, SUB, HG_DIM)
    q3 = q.reshape(n, SUB, HG_DIM)
    k3 = k.reshape(n, SUB, HG_DIM)
    kt = (k3 * jnp.exp(b3[:, SUB - 1:SUB, :] - b3)).reshape(c, HG_DIM).astype(BF16)

    row = lax.broadcasted_iota(jnp.int32, (c, 1), 0)
    a = None
    if n > 1:
        qs, ks = [], []
        zero = jnp.zeros_like(kt)
        for j in range(n - 1):
            ref = b[SUB * j + SUB - 1:SUB * j + SUB, :]
            qj = q * jnp.exp(jnp.minimum(b - ref, 0.0))
            qs.append(jnp.where(row >= SUB * (j + 1), qj, 0.0).astype(BF16))
            ks.append(jnp.where((row >= SUB * j) & (row < SUB * (j + 1)), kt, zero))
        a = _dot_nt(jnp.concatenate(qs, axis=1), jnp.concatenate(ks, axis=1))

    sub = lax.broadcasted_iota(jnp.int32, (1, SUB, 1), 1)
    ps = []
    for s in range(SUB):
        e = jnp.exp(jnp.minimum(b3 - b3[:, s:s + 1, :], 0.0))
        p = q3 * e * k3[:, s:s + 1, :]
        ps.append(jnp.where(sub >= s, p, 0.0).reshape(c, HG_DIM).astype(BF16))
    sums = _dot(jnp.concatenate(ps, axis=0), ones)
    colid = lax.broadcasted_iota(jnp.int32, (c, c), 1)
    base = (lax.broadcasted_iota(jnp.int32, (c, c), 0) // SUB) * SUB
    for s in range(SUB):
        d = jnp.where(colid == base + s, sums[s * c:(s + 1) * c, :], 0.0)
        a = d if a is None else a + d

    o = _dot_nt((q * jnp.exp(b)).astype(BF16), st.astype(BF16)) + _dot(a.astype(BF16), v)
    khat = (k * jnp.exp(b_last - b)).astype(BF16)
    st_new = st * jnp.exp(b_last) + _dot_tn(v, khat)
    return o, st_new


def _hgrn_kernel(*refs, c, n_chunks, has_s0):
    if has_s0:
        (qh_ref, kin_ref, lf_ref, hi_ref, gt_ref, gain_ref, low_ref, ones_ref, s0_ref,
         o_ref, s_ref, st_ref) = refs
    else:
        (qh_ref, kin_ref, lf_ref, hi_ref, gt_ref, gain_ref, low_ref, ones_ref,
         o_ref, s_ref, st_ref) = refs
    step = pl.program_id(1)

    @pl.when(step == 0)
    def _():
        for h in range(HG_HEADS):
            st_ref[h] = s0_ref[0, h].T if has_s0 else jnp.zeros((HG_DIM, HG_DIM), F32)

    lower = low_ref[...]
    ones = ones_ref[...]
    gain = gain_ref[...]

    def chunk(ci, carry):
        r0 = pl.multiple_of(ci * c, c)
        for h in range(HG_HEADS):
            ln = slice(h * HG_DIM, (h + 1) * HG_DIM)
            o, st = _hgrn_chunk(qh_ref[pl.ds(r0, c), ln], kin_ref[pl.ds(r0, c), ln],
                                lf_ref[pl.ds(r0, c), ln], hi_ref[pl.ds(r0, c), ln],
                                st_ref[h], lower, ones, c)
            st_ref[h] = st
            y = _rms_rows(o, gain) * gt_ref[pl.ds(r0, c), ln]
            o_ref[pl.ds(r0, c), ln] = y.astype(o_ref.dtype)
        return carry

    lax.fori_loop(0, n_chunks, chunk, 0)

    @pl.when(step == pl.num_programs(1) - 1)
    def _():
        for h in range(HG_HEADS):
            s_ref[0, h] = st_ref[h].T


def _hgrn(qh, kin, lf, hi, gate, gain, s0, *, batch, c, rb):
    rows, w = qh.shape
    t = rows // batch
    nsteps = t // rb
    row_spec = pl.BlockSpec((rb, w), lambda b, s: (b * nsteps + s, 0))
    lower = jnp.tril(jnp.ones((c, c), BF16))
    ones = jnp.ones((HG_DIM, c), BF16)
    state_spec = pl.BlockSpec((1, HG_HEADS, HG_DIM, HG_DIM), lambda b, s: (b, 0, 0, 0))
    in_specs = [row_spec] * 5 + [pl.BlockSpec((1, HG_DIM), lambda b, s: (0, 0)),
                                 pl.BlockSpec((c, c), lambda b, s: (0, 0)),
                                 pl.BlockSpec((HG_DIM, c), lambda b, s: (0, 0))]
    args = [qh, kin, lf, hi, gate, gain.reshape(1, HG_DIM), lower, ones]
    if s0 is not None:
        in_specs.append(state_spec)
        args.append(s0)
    return pl.pallas_call(
        functools.partial(_hgrn_kernel, c=c, n_chunks=rb // c, has_s0=s0 is not None),
        grid=(batch, nsteps),
        in_specs=in_specs,
        out_specs=[row_spec, state_spec],
        out_shape=[jax.ShapeDtypeStruct((rows, w), BF16),
                   jax.ShapeDtypeStruct((batch, HG_HEADS, HG_DIM, HG_DIM), F32)],
        scratch_shapes=[pltpu.VMEM((HG_HEADS, HG_DIM, HG_DIM), F32)],
        compiler_params=_params(2),
        name="hgrn_s0" if s0 is not None else "hgrn",
    )(*args)


def _tile_sizes(rows, t):
    tm = min(1024, rows)
    tq = min(256, t)
    c = min(64, t)
    rb = min(512, t)
    return tm, tq, c, rb


def _column_chunks(w, fc):
    d, n = w.shape
    return w.reshape(d, n // fc, fc).transpose(1, 0, 2)


def _suffix_ones(n):
    r = lax.broadcasted_iota(jnp.int32, (n, n), 0)
    col = lax.broadcasted_iota(jnp.int32, (n, n), 1)
    return (r >= col).astype(BF16)


def _layer(x, wts, lb, *, batch, cache=None, s0=None):
    rows, _ = x.shape
    t = rows // batch
    tm, tq, c, rb = _tile_sizes(rows, t)
    fc = FFN_CHUNK
    x1 = _ffn(x, wts["ffn1_norm"], wts["ffn1_w_in"], wts["ffn1_w_out"], tm=tm, fc=fc)
    (kf, vf, qb, kb, vb, qh, kin, lf, hi, gate) = _proj(
        x1, wts["mix_norm"], wts["w_in"], wts["sb_q_gain"], wts["sb_k_gain"], lb,
        wts["head_mean"], tm=min(512, rows), batch=batch)
    gain_pair = jnp.tile(wts["sb_out_gain"], LANES // SB_HEAD_DIM).reshape(1, LANES)
    if cache is None:
        sb_o = _sb_prompt(qb, kb, vb, gain_pair, _suffix_ones(tq), batch=batch, tq=tq)
    else:
        tk = min(256, cache[0].shape[1])
        sb_o = _sb_sample(qb, kb, vb, cache[0], cache[1], gain_pair, _suffix_ones(tk),
                          _suffix_ones(t), batch=batch, tk=tk)
    hg_o, state = _hgrn(qh, kin, lf, hi, gate, wts["hg_out_gain"], s0, batch=batch, c=c, rb=rb)
    w_mix = wts["w_out"]
    x3 = _ffn(x1, wts["ffn2_norm"], wts["ffn2_w_in"], wts["ffn2_w_out"],
              mix=(sb_o, hg_o, w_mix[:SB_WIDTH], w_mix[SB_WIDTH:]), tm=tm, fc=fc)
    return x3, kf, vf, state


def _heads(a, batch, t):
    if a.ndim == 3:
        return a.reshape(batch, SB_HEADS, SB_HEAD_DIM, t).transpose(0, 3, 1, 2)
    return a.reshape(batch, t, SB_HEADS, SB_HEAD_DIM)


def kernel(x_prompt, x_sample, cache_sb_k, cache_sb_v, state_hgrn, ffn1_norm, ffn1_w_in,
           ffn1_w_out, mix_norm, w_in, sb_q_gain, sb_k_gain, hg_lb_logits, sb_out_gain,
           hg_out_gain, w_out, ffn2_norm, ffn2_w_in, ffn2_w_out):
    depth = w_in.shape[0]
    bp, tp, d = x_prompt.shape
    bs, ts, _ = x_sample.shape
    past = cache_sb_k.shape[2]
    lb_all = jnp.cumsum(jax.nn.softmax(hg_lb_logits.astype(F32), axis=0), axis=0)
    hid = lax.broadcasted_iota(jnp.int32, (SB_WIDTH, SB_WIDTH), 0) // SB_HEAD_DIM
    hid_t = lax.broadcasted_iota(jnp.int32, (SB_WIDTH, SB_WIDTH), 1) // SB_HEAD_DIM
    head_mean = jnp.where(hid == hid_t, 1.0 / SB_HEAD_DIM, 0.0).astype(BF16)

    xp = x_prompt.reshape(bp * tp, d)
    xs = x_sample.reshape(bs * ts, d)
    outs = [[] for _ in range(6)]
    for l in range(depth):
        wts = dict(
            ffn1_norm=ffn1_norm[l],
            ffn1_w_in=_column_chunks(ffn1_w_in[l].astype(BF16), FFN_CHUNK),
            ffn1_w_out=ffn1_w_out[l].astype(BF16), mix_norm=mix_norm[l],
            w_in=w_in[l].astype(BF16), sb_q_gain=sb_q_gain[l], sb_k_gain=sb_k_gain[l],
            sb_out_gain=sb_out_gain[l], hg_out_gain=hg_out_gain[l],
            w_out=w_out[l].astype(BF16), ffn2_norm=ffn2_norm[l],
            ffn2_w_in=_column_chunks(ffn2_w_in[l].astype(BF16), FFN_CHUNK),
            ffn2_w_out=ffn2_w_out[l].astype(BF16),
            head_mean=head_mean)
        xp, kp, vp, sp = _layer(xp, wts, lb_all[l], batch=bp)
        cache = tuple(c[l].transpose(0, 2, 3, 1).reshape(bs * SB_WIDTH, past)
                      for c in (cache_sb_k, cache_sb_v))
        xs, kn, vn, sn = _layer(xs, wts, lb_all[l], batch=bs, cache=cache, s0=state_hgrn[l])
        for lst, val in zip(outs, (_heads(kp, bp, tp), _heads(vp, bp, tp), sp,
                                   _heads(kn, bs, ts), _heads(vn, bs, ts), sn)):
            lst.append(val)
    return (xp.reshape(bp, tp, d), xs.reshape(bs, ts, d)) + tuple(
        jnp.stack(lst, axis=0) for lst in outs)
```

```python
import functools

import jax
import jax.numpy as jnp
from jax import lax
from jax.experimental import pallas as pl
from jax.experimental.pallas import tpu as pltpu

EPS = 1e-6
SB_HEADS = 8
SB_HEAD_DIM = 64
SB_WIDTH = SB_HEADS * SB_HEAD_DIM
SB_SCALE = SB_HEAD_DIM ** -0.5
LOG2_E = 1.4426950408889634
EXP2_CLAMP = 126.0
HG_HEADS = 4
HG_DIM = 128
HG_WIDTH = HG_HEADS * HG_DIM
LANES = 128
SUB = 8
HEAD_PAIRS = SB_WIDTH // LANES

BF16 = jnp.bfloat16
F32 = jnp.float32

VMEM_LIMIT = 56 * 1024 * 1024
FFN_CHUNK = 256
HGRN_CHUNKS_IN_FLIGHT = 2


def _dot(a, b):
    return jnp.dot(a, b, preferred_element_type=F32)


def _dot_nt(a, b):
    return lax.dot_general(a, b, (((1,), (1,)), ((), ())), preferred_element_type=F32)


def _dot_tn(a, b):
    return lax.dot_general(a, b, (((0,), (0,)), ((), ())), preferred_element_type=F32)


def _sigmoid(x):
    return 1.0 / (1.0 + jnp.exp(-x))


def _rms_rows(x, g):
    ms = jnp.mean(x * x, axis=-1, keepdims=True)
    return x * lax.rsqrt(ms + EPS) * g


def _params(n_grid):
    return pltpu.CompilerParams(
        dimension_semantics=("arbitrary",) * n_grid, vmem_limit_bytes=VMEM_LIMIT)


def _ffn_kernel(*refs, with_mix):
    if with_mix:
        (x_ref, ma_ref, mb_ref, woa_ref, wob_ref, g_ref, wa_ref, wb_ref, wo_ref,
         o_ref, h_ref) = refs
    else:
        x_ref, g_ref, wa_ref, wb_ref, wo_ref, o_ref, h_ref = refs

    @pl.when(pl.program_id(1) == 0)
    def _():
        x = x_ref[...]
        if with_mix:
            x = x + _dot(ma_ref[...], woa_ref[...]) + _dot(mb_ref[...], wob_ref[...])
        o_ref[...] = x
        h_ref[...] = _rms_rows(x, g_ref[...]).astype(BF16)

    h = h_ref[...]
    a = _dot(h, wa_ref[0])
    b = _dot(h, wb_ref[0])
    act = a * (0.5 * _sigmoid(a)) * b
    o_ref[...] += _dot(act.astype(BF16), wo_ref[...])


def _ffn(x, norm_g, w_in, w_out, mix=None, *, tm, fc):
    rows, d = x.shape
    nj = w_out.shape[0] // fc
    row_spec = pl.BlockSpec((tm, d), lambda i, j: (i, 0))
    in_specs = [row_spec]
    args = [x]
    if mix is not None:
        ma, mb, woa, wob = mix
        in_specs += [pl.BlockSpec((tm, ma.shape[1]), lambda i, j: (i, 0)),
                     pl.BlockSpec((tm, mb.shape[1]), lambda i, j: (i, 0)),
                     pl.BlockSpec(woa.shape, lambda i, j: (0, 0)),
                     pl.BlockSpec(wob.shape, lambda i, j: (0, 0))]
        args += [ma, mb, woa, wob]
    in_specs += [pl.BlockSpec((1, d), lambda i, j: (0, 0)),
                 pl.BlockSpec((1, d, fc), lambda i, j: (j, 0, 0)),
                 pl.BlockSpec((1, d, fc), lambda i, j: (j + nj, 0, 0)),
                 pl.BlockSpec((fc, d), lambda i, j: (j, 0))]
    args += [norm_g.reshape(1, d), w_in, w_in, w_out]
    return pl.pallas_call(
        functools.partial(_ffn_kernel, with_mix=mix is not None),
        grid=(rows // tm, nj),
        in_specs=in_specs,
        out_specs=row_spec,
        out_shape=jax.ShapeDtypeStruct((rows, d), F32),
        scratch_shapes=[pltpu.VMEM((tm, d), BF16)],
        compiler_params=_params(2),
        name="ffn_mix" if mix is not None else "ffn",
    )(*args)


def _proj_kernel(x_ref, g_ref, w_ref, qg_ref, kg_ref, lb_ref, hm_ref,
                 kf_ref, vf_ref, qb_ref, kb_ref, vb_ref,
                 qh_ref, kin_ref, lf_ref, hi_ref, gt_ref, *, feature_major):
    h = _rms_rows(x_ref[...], g_ref[...]).astype(BF16)
    w = SB_WIDTH

    def col(c):
        return _dot(h, w_ref[:, c * w:(c + 1) * w])

    def head_norm(a, gain):
        ms = _dot((a * a).astype(BF16), hm_ref[...])
        return a * lax.rsqrt(ms + EPS) * gain

    qn = head_norm(col(0), qg_ref[...])
    qb_ref[...] = (qn * (SB_SCALE * LOG2_E)).astype(BF16)
    kn = head_norm(col(1), kg_ref[...])
    sv = col(2)
    if feature_major:
        kf_ref[0] = kn.T
        vf_ref[0] = sv.T
    else:
        kf_ref[...] = kn
        vf_ref[...] = sv
    kb_ref[...] = kn.astype(BF16)
    vb_ref[...] = sv.astype(BF16)

    hq = col(3)
    qh_ref[...] = hq * _sigmoid(hq)
    hf = col(4)
    lb = lb_ref[...]
    f = lb + (1.0 - lb) * _sigmoid(hf)
    lf_ref[...] = jnp.log2(f)
    kin_ref[...] = (1.0 - lb) * _sigmoid(-hf)
    hi_ref[...] = col(5).astype(BF16)
    hg = col(6)
    gt_ref[...] = hg * _sigmoid(hg)


def _proj(x, norm_g, w_in, q_gain, k_gain, lb, head_mean, *, tm, batch):
    rows, d = x.shape
    t = rows // batch
    w = SB_WIDTH
    feature_major = t % tm == 0
    row_spec = pl.BlockSpec((tm, w), lambda i: (i, 0))
    vec = lambda n: pl.BlockSpec((1, n), lambda i: (0, 0))
    f32o = jax.ShapeDtypeStruct((rows, w), F32)
    bf16o = jax.ShapeDtypeStruct((rows, w), BF16)
    if feature_major:
        nt = t // tm
        kv_spec = pl.BlockSpec((1, w, tm), lambda i: (i // nt, 0, i % nt))
        kv_shape = jax.ShapeDtypeStruct((batch, w, t), F32)
    else:
        kv_spec, kv_shape = row_spec, f32o
    return pl.pallas_call(
        functools.partial(_proj_kernel, feature_major=feature_major),
        grid=(rows // tm,),
        in_specs=[pl.BlockSpec((tm, d), lambda i: (i, 0)), vec(d),
                  pl.BlockSpec(w_in.shape, lambda i: (0, 0)),
                  vec(w), vec(w), vec(w),
                  pl.BlockSpec(head_mean.shape, lambda i: (0, 0))],
        out_specs=[kv_spec, kv_spec] + [row_spec] * 8,
        out_shape=[kv_shape, kv_shape, bf16o, bf16o, bf16o, f32o, f32o, f32o, bf16o, f32o],
        compiler_params=_params(1),
        name="proj",
    )(x, norm_g.reshape(1, d), w_in,
      jnp.tile(q_gain, SB_HEADS).reshape(1, w), jnp.tile(k_gain, SB_HEADS).reshape(1, w),
      lb.reshape(1, w), head_mean)


def _sb_tiles(ops, upper, mask, feature_major=False):
    scores, weighted = (_dot, _dot_nt) if feature_major else (_dot_nt, _dot)
    zs = [scores(q, kblk) for q, kblk, _ in ops]
    suffixes = []
    for z in zs:
        drop = jnp.maximum(z, jnp.log2(1.0 + jnp.exp2(jnp.minimum(z, EXP2_CLAMP))))
        if mask is not None:
            drop = jnp.where(mask, drop, 0.0)
        suffixes.append(_dot(drop.astype(BF16), upper))
    out = []
    for z, suffix, (_, _, vblk) in zip(zs, suffixes, ops):
        w = jnp.exp2(z - suffix)
        if mask is not None:
            w = jnp.where(mask, w, 0.0)
        out.append((weighted(w.astype(BF16), vblk), suffix[:, :1]))
    return out


def _sb_finish(acc0, acc1, gain):
    lane = lax.broadcasted_iota(jnp.int32, (1, LANES), 1)
    first = lane < SB_HEAD_DIM
    o = jnp.where(first, acc0, acc1)
    sq = o * o
    s0 = jnp.sum(jnp.where(first, sq, 0.0), axis=-1, keepdims=True)
    s1 = jnp.sum(jnp.where(first, 0.0, sq), axis=-1, keepdims=True)
    ms = jnp.where(first, s0, s1) * (1.0 / SB_HEAD_DIM)
    return o * lax.rsqrt(ms + EPS) * gain


def _head_masked(q):
    lane = lax.broadcasted_iota(jnp.int32, (1, LANES), 1)
    first = lane < SB_HEAD_DIM
    zero = jnp.zeros_like(q)
    return jnp.where(first, q, zero), jnp.where(first, zero, q)


def _sb_prompt_kernel(q_ref, k_ref, v_ref, gain_ref, up_ref, o_ref, acc_ref, car_ref, *, tq):
    i = pl.program_id(2)
    qms = _head_masked(q_ref[...])
    upper = up_ref[...]
    row = lax.broadcasted_iota(jnp.int32, (tq, tq), 0)
    colid = lax.broadcasted_iota(jnp.int32, (tq, tq), 1)
    causal = colid < row

    def blocks(kbs, mask):
        ops = []
        for kb in kbs:
            start = pl.multiple_of(kb * tq, tq)
            kblk = k_ref[pl.ds(start, tq), :]
            vblk = v_ref[pl.ds(start, tq), :]
            ops += [(qms[0], kblk, vblk), (qms[1], kblk, vblk)]
        res = _sb_tiles(ops, upper, mask)
        for h in range(2):
            carry = car_ref[h]
            acc = acc_ref[h]
            for pv, total in res[h::2]:
                acc = acc + jnp.exp2(-carry) * pv
                carry = carry + total
            car_ref[h] = carry
            acc_ref[h] = acc

    acc_ref[...] = jnp.zeros_like(acc_ref)
    car_ref[...] = jnp.zeros_like(car_ref)
    blocks([i], causal)

    @pl.when(i % 2 == 1)
    def _():
        blocks([i - 1], None)

    n_pairs = i // 2

    def pair(j, carry):
        top = 2 * (n_pairs - j) - 1
        blocks([top, top - 1], None)
        return carry

    lax.fori_loop(0, n_pairs, pair, 0)
    o_ref[...] = _sb_finish(acc_ref[0], acc_ref[1], gain_ref[...]).astype(o_ref.dtype)


def _sb_prompt(q, k, v, gain_pair, upper, *, batch, tq):
    rows, w = q.shape
    t = rows // batch
    nq = t // tq
    qspec = pl.BlockSpec((tq, LANES), lambda b, p, i: (b * nq + i, p))
    kvspec = pl.BlockSpec((t, LANES), lambda b, p, i: (b, p))
    return pl.pallas_call(
        functools.partial(_sb_prompt_kernel, tq=tq),
        grid=(batch, HEAD_PAIRS, nq),
        in_specs=[qspec, kvspec, kvspec,
                  pl.BlockSpec((1, LANES), lambda b, p, i: (0, 0)),
                  pl.BlockSpec((tq, tq), lambda b, p, i: (0, 0))],
        out_specs=qspec,
        out_shape=jax.ShapeDtypeStruct((rows, w), BF16),
        scratch_shapes=[pltpu.VMEM((2, tq, LANES), F32), pltpu.VMEM((2, tq, 1), F32)],
        compiler_params=_params(3),
        name="sb_prompt",
    )(q, k, v, gain_pair, upper)


def _sb_sample_kernel(q_ref, kn_ref, vn_ref, ck_ref, cv_ref, gain_ref, up_ref, upn_ref,
                      o_ref, *, tk, n_blocks, t_new):
    qms = _head_masked(q_ref[...])
    upper = up_ref[...]
    row = lax.broadcasted_iota(jnp.int32, (t_new, t_new), 0)
    colid = lax.broadcasted_iota(jnp.int32, (t_new, t_new), 1)
    causal = colid < row
    kn = kn_ref[...]
    vn = vn_ref[...]

    res = _sb_tiles([(qm, kn, vn) for qm in qms], upn_ref[...], causal)
    accs = [r[0] for r in res]
    carries = [r[1] for r in res]
    ops = []
    for kb in range(n_blocks - 1, -1, -1):
        keys = slice(kb * tk, (kb + 1) * tk)
        kblk = ck_ref[:, keys].astype(BF16)
        vblk = cv_ref[:, keys].astype(BF16)
        ops += [(qm, kblk, vblk) for qm in qms]
    for n, (pv, total) in enumerate(_sb_tiles(ops, upper, None, feature_major=True)):
        h = n % 2
        accs[h] = accs[h] + jnp.exp2(-carries[h]) * pv
        carries[h] = carries[h] + total
    o_ref[...] = _sb_finish(accs[0], accs[1], gain_ref[...]).astype(o_ref.dtype)


def _sb_sample(q, k_new, v_new, cache_k, cache_v, gain_pair, upper, upper_new, *, batch, tk):
    rows, w = q.shape
    t_new = rows // batch
    past = cache_k.shape[1]
    qspec = pl.BlockSpec((t_new, LANES), lambda b, p: (b, p))
    cspec = pl.BlockSpec((LANES, past), lambda b, p: (b * HEAD_PAIRS + p, 0))
    return pl.pallas_call(
        functools.partial(_sb_sample_kernel, tk=tk, n_blocks=past // tk, t_new=t_new),
        grid=(batch, HEAD_PAIRS),
        in_specs=[qspec, qspec, qspec, cspec, cspec,
                  pl.BlockSpec((1, LANES), lambda b, p: (0, 0)),
                  pl.BlockSpec((tk, tk), lambda b, p: (0, 0)),
                  pl.BlockSpec((t_new, t_new), lambda b, p: (0, 0))],
        out_specs=qspec,
        out_shape=jax.ShapeDtypeStruct((rows, w), BF16),
        compiler_params=_params(2),
        name="sb_sample",
    )(q, k_new, v_new, cache_k, cache_v, gain_pair, upper, upper_new)


def _hgrn_chains(chains, lower, ones, c):
    n = c // SUB
    pieces = []
    for _, _, g, _ in chains:
        g1 = g.astype(BF16)
        r1 = g - g1.astype(F32)
        g2 = r1.astype(BF16)
        g3 = (r1 - g2.astype(F32)).astype(BF16)
        pieces.append(jnp.concatenate([g1, g2, g3], axis=1))
    sums3 = [_dot(lower, p) for p in pieces]

    row = lax.broadcasted_iota(jnp.int32, (c, 1), 0)
    sub = lax.broadcasted_iota(jnp.int32, (1, SUB, 1), 1)
    mids = []
    for (q, k, _, _), bs in zip(chains, sums3):
        b = bs[:, :HG_DIM] + (bs[:, HG_DIM:2 * HG_DIM] + bs[:, 2 * HG_DIM:])
        b_last = b[c - 1:c, :]
        b3 = b.reshape(n, SUB, HG_DIM)
        q3 = q.reshape(n, SUB, HG_DIM)
        k3 = k.reshape(n, SUB, HG_DIM)
        kt = (k3 * jnp.exp2(b3[:, SUB - 1:SUB, :] - b3)).reshape(c, HG_DIM).astype(BF16)
        qs, ks = [], []
        zero = jnp.zeros_like(kt)
        for j in range(n - 1):
            last = SUB * (j + 1)
            qj = q[last:] * jnp.exp2(b[last:] - b[last - 1:last])
            qs.append(jnp.concatenate([jnp.zeros((last, HG_DIM), F32), qj]).astype(BF16))
            ks.append(jnp.where((row >= SUB * j) & (row < last), kt, zero))
        ps = []
        for s in range(SUB):
            e = jnp.exp2(jnp.minimum(b3 - b3[:, s:s + 1, :], 0.0))
            p = q3 * e * k3[:, s:s + 1, :]
            ps.append(jnp.where(sub >= s, p, 0.0).reshape(c, HG_DIM).astype(BF16))
        q0 = (q * jnp.exp2(b)).astype(BF16)
        khat = (k * jnp.exp2(b_last - b)).astype(BF16)
        mids.append((qs, ks, jnp.concatenate(ps, axis=0), q0, khat, jnp.exp2(b_last)))

    far = [_dot_nt(jnp.concatenate(qs, axis=1), jnp.concatenate(ks, axis=1)) if n > 1 else None
           for qs, ks, *_ in mids]
    near = [_dot(m[2], ones) for m in mids]
    colid = lax.broadcasted_iota(jnp.int32, (c, c), 1)
    base = (lax.broadcasted_iota(jnp.int32, (c, c), 0) // SUB) * SUB
    atts = []
    for a, sums in zip(far, near):
        for s in range(SUB):
            d = jnp.where(colid == base + s, sums[s * c:(s + 1) * c, :], 0.0)
            a = d if a is None else a + d
        atts.append(a.astype(BF16))
    avs = [_dot(a, ch[3]) for a, ch in zip(atts, chains)]
    kvs = [_dot_tn(ch[3], m[4]) for ch, m in zip(chains, mids)]
    return [(m[3], av, kv, m[5]) for m, av, kv in zip(mids, avs, kvs)]


def _hgrn_kernel(*refs, c, n_chunks, has_s0):
    if has_s0:
        (qh_ref, kin_ref, lf_ref, hi_ref, gt_ref, gain_ref, low_ref, ones_ref, s0_ref,
         o_ref, s_ref, st_ref) = refs
    else:
        (qh_ref, kin_ref, lf_ref, hi_ref, gt_ref, gain_ref, low_ref, ones_ref,
         o_ref, s_ref, st_ref) = refs
    step = pl.program_id(1)

    @pl.when(step == 0)
    def _():
        for h in range(HG_HEADS):
            st_ref[h] = s0_ref[0, h].T if has_s0 else jnp.zeros((HG_DIM, HG_DIM), F32)

    lower = low_ref[...]
    ones = ones_ref[...]
    gain = gain_ref[...]
    group = min(HGRN_CHUNKS_IN_FLIGHT, n_chunks)

    def chunks(ci, carry):
        where = []
        for u in range(group):
            r0 = pl.multiple_of((ci * group + u) * c, c)
            where += [(pl.ds(r0, c), slice(h * HG_DIM, (h + 1) * HG_DIM))
                      for h in range(HG_HEADS)]
        parts = _hgrn_chains(
            [(qh_ref[rows, ln], kin_ref[rows, ln], lf_ref[rows, ln], hi_ref[rows, ln])
             for rows, ln in where], lower, ones, c)
        for n, ((rows, ln), (q0, av, kv, decay)) in enumerate(zip(where, parts)):
            h = n % HG_HEADS
            st = st_ref[h]
            o = _dot_nt(q0, st.astype(BF16)) + av
            st_ref[h] = st * decay + kv
            y = _rms_rows(o, gain) * gt_ref[rows, ln]
            o_ref[rows, ln] = y.astype(o_ref.dtype)
        return carry

    lax.fori_loop(0, n_chunks // group, chunks, 0)

    @pl.when(step == pl.num_programs(1) - 1)
    def _():
        for h in range(HG_HEADS):
            s_ref[0, h] = st_ref[h].T


def _hgrn(qh, kin, lf, hi, gate, gain, s0, *, batch, c, rb):
    rows, w = qh.shape
    t = rows // batch
    nsteps = t // rb
    row_spec = pl.BlockSpec((rb, w), lambda b, s: (b * nsteps + s, 0))
    lower = jnp.tril(jnp.ones((c, c), BF16))
    ones = jnp.ones((HG_DIM, c), BF16)
    state_spec = pl.BlockSpec((1, HG_HEADS, HG_DIM, HG_DIM), lambda b, s: (b, 0, 0, 0))
    in_specs = [row_spec] * 5 + [pl.BlockSpec((1, HG_DIM), lambda b, s: (0, 0)),
                                 pl.BlockSpec((c, c), lambda b, s: (0, 0)),
                                 pl.BlockSpec((HG_DIM, c), lambda b, s: (0, 0))]
    args = [qh, kin, lf, hi, gate, gain.reshape(1, HG_DIM), lower, ones]
    if s0 is not None:
        in_specs.append(state_spec)
        args.append(s0)
    return pl.pallas_call(
        functools.partial(_hgrn_kernel, c=c, n_chunks=rb // c, has_s0=s0 is not None),
        grid=(batch, nsteps),
        in_specs=in_specs,
        out_specs=[row_spec, state_spec],
        out_shape=[jax.ShapeDtypeStruct((rows, w), BF16),
                   jax.ShapeDtypeStruct((batch, HG_HEADS, HG_DIM, HG_DIM), F32)],
        scratch_shapes=[pltpu.VMEM((HG_HEADS, HG_DIM, HG_DIM), F32)],
        compiler_params=_params(2),
        name="hgrn_s0" if s0 is not None else "hgrn",
    )(*args)


def _tile_sizes(rows, t):
    tm = min(1024, rows)
    tq = min(256, t)
    c = min(64, t)
    rb = min(512, t)
    return tm, tq, c, rb


def _column_chunks(w, fc):
    d, n = w.shape
    return w.reshape(d, n // fc, fc).transpose(1, 0, 2)


def _suffix_ones(n):
    r = lax.broadcasted_iota(jnp.int32, (n, n), 0)
    col = lax.broadcasted_iota(jnp.int32, (n, n), 1)
    return (r >= col).astype(BF16)


def _layer(x, wts, lb, *, batch, cache=None, s0=None):
    rows, _ = x.shape
    t = rows // batch
    tm, tq, c, rb = _tile_sizes(rows, t)
    fc = FFN_CHUNK
    x1 = _ffn(x, wts["ffn1_norm"], wts["ffn1_w_in"], wts["ffn1_w_out"], tm=tm, fc=fc)
    (kf, vf, qb, kb, vb, qh, kin, lf, hi, gate) = _proj(
        x1, wts["mix_norm"], wts["w_in"], wts["sb_q_gain"], wts["sb_k_gain"], lb,
        wts["head_mean"], tm=min(512, rows), batch=batch)
    gain_pair = jnp.tile(wts["sb_out_gain"], LANES // SB_HEAD_DIM).reshape(1, LANES)
    if cache is None:
        sb_o = _sb_prompt(qb, kb, vb, gain_pair, _suffix_ones(tq), batch=batch, tq=tq)
    else:
        tk = min(256, cache[0].shape[1])
        sb_o = _sb_sample(qb, kb, vb, cache[0], cache[1], gain_pair, _suffix_ones(tk),
                          _suffix_ones(t), batch=batch, tk=tk)
    hg_o, state = _hgrn(qh, kin, lf, hi, gate, wts["hg_out_gain"], s0, batch=batch, c=c, rb=rb)
    w_mix = wts["w_out"]
    x3 = _ffn(x1, wts["ffn2_norm"], wts["ffn2_w_in"], wts["ffn2_w_out"],
              mix=(sb_o, hg_o, w_mix[:SB_WIDTH], w_mix[SB_WIDTH:]), tm=tm, fc=fc)
    return x3, kf, vf, state


def _heads(a, batch, t):
    if a.ndim == 3:
        return a.reshape(batch, SB_HEADS, SB_HEAD_DIM, t).transpose(0, 3, 1, 2)
    return a.reshape(batch, t, SB_HEADS, SB_HEAD_DIM)


def kernel(x_prompt, x_sample, cache_sb_k, cache_sb_v, state_hgrn, ffn1_norm, ffn1_w_in,
           ffn1_w_out, mix_norm, w_in, sb_q_gain, sb_k_gain, hg_lb_logits, sb_out_gain,
           hg_out_gain, w_out, ffn2_norm, ffn2_w_in, ffn2_w_out):
    depth = w_in.shape[0]
    bp, tp, d = x_prompt.shape
    bs, ts, _ = x_sample.shape
    past = cache_sb_k.shape[2]
    lb_all = jnp.cumsum(jax.nn.softmax(hg_lb_logits.astype(F32), axis=0), axis=0)
    hid = lax.broadcasted_iota(jnp.int32, (SB_WIDTH, SB_WIDTH), 0) // SB_HEAD_DIM
    hid_t = lax.broadcasted_iota(jnp.int32, (SB_WIDTH, SB_WIDTH), 1) // SB_HEAD_DIM
    head_mean = jnp.where(hid == hid_t, 1.0 / SB_HEAD_DIM, 0.0).astype(BF16)

    xp = x_prompt.reshape(bp * tp, d)
    xs = x_sample.reshape(bs * ts, d)
    outs = [[] for _ in range(6)]
    for l in range(depth):
        wts = dict(
            ffn1_norm=ffn1_norm[l],
            ffn1_w_in=_column_chunks(ffn1_w_in[l].astype(BF16), FFN_CHUNK),
            ffn1_w_out=ffn1_w_out[l].astype(BF16), mix_norm=mix_norm[l],
            w_in=w_in[l].astype(BF16), sb_q_gain=sb_q_gain[l], sb_k_gain=sb_k_gain[l],
            sb_out_gain=sb_out_gain[l], hg_out_gain=hg_out_gain[l],
            w_out=w_out[l].astype(BF16), ffn2_norm=ffn2_norm[l],
            ffn2_w_in=_column_chunks(ffn2_w_in[l].astype(BF16), FFN_CHUNK),
            ffn2_w_out=ffn2_w_out[l].astype(BF16),
            head_mean=head_mean)
        xp, kp, vp, sp = _layer(xp, wts, lb_all[l], batch=bp)
        cache = tuple(c[l].transpose(0, 2, 3, 1).reshape(bs * SB_WIDTH, past)
                      for c in (cache_sb_k, cache_sb_v))
        xs, kn, vn, sn = _layer(xs, wts, lb_all[l], batch=bs, cache=cache, s0=state_hgrn[l])
        for lst, val in zip(outs, (_heads(kp, bp, tp), _heads(vp, bp, tp), sp,
                                   _heads(kn, bs, ts), _heads(vn, bs, ts), sn)):
            lst.append(val)
    return (xp.reshape(bp, tp, d), xs.reshape(bs, ts, d)) + tuple(
        jnp.stack(lst, axis=0) for lst in outs)
```

```python
import functools

import jax
import jax.numpy as jnp
from jax import lax
from jax.experimental import pallas as pl
from jax.experimental.pallas import tpu as pltpu

EPS = 1e-6
SB_HEADS = 8
SB_HEAD_DIM = 64
SB_WIDTH = SB_HEADS * SB_HEAD_DIM
SB_SCALE = SB_HEAD_DIM ** -0.5
LOG2_E = 1.4426950408889634
EXP2_CLAMP = 126.0
HG_HEADS = 4
HG_DIM = 128
HG_WIDTH = HG_HEADS * HG_DIM
LANES = 128
SUB = 8
HEAD_PAIRS = SB_WIDTH // LANES

BF16 = jnp.bfloat16
F32 = jnp.float32

VMEM_LIMIT = 56 * 1024 * 1024
FFN_CHUNK = 256
SB_BLOCKS_IN_FLIGHT = 4
HGRN_CHUNKS_IN_FLIGHT = 2


def _dot(a, b):
    return jnp.dot(a, b, preferred_element_type=F32)


def _dot_nt(a, b):
    return lax.dot_general(a, b, (((1,), (1,)), ((), ())), preferred_element_type=F32)


def _dot_tn(a, b):
    return lax.dot_general(a, b, (((0,), (0,)), ((), ())), preferred_element_type=F32)


def _sigmoid(x):
    return 1.0 / (1.0 + jnp.exp(-x))


def _rms_rows(x, g):
    ms = jnp.mean(x * x, axis=-1, keepdims=True)
    return x * lax.rsqrt(ms + EPS) * g


def _params(n_grid):
    return pltpu.CompilerParams(
        dimension_semantics=("arbitrary",) * n_grid, vmem_limit_bytes=VMEM_LIMIT)


def _ffn_kernel(*refs, with_mix):
    if with_mix:
        (x_ref, ma_ref, mb_ref, woa_ref, wob_ref, g_ref, wa_ref, wb_ref, wo_ref,
         o_ref, h_ref) = refs
    else:
        x_ref, g_ref, wa_ref, wb_ref, wo_ref, o_ref, h_ref = refs

    @pl.when(pl.program_id(1) == 0)
    def _():
        x = x_ref[...]
        if with_mix:
            x = x + _dot(ma_ref[...], woa_ref[...]) + _dot(mb_ref[...], wob_ref[...])
        o_ref[...] = x
        h_ref[...] = _rms_rows(x, g_ref[...]).astype(BF16)

    h = h_ref[...]
    a = _dot(h, wa_ref[0])
    b = _dot(h, wb_ref[0])
    act = a * (0.5 * _sigmoid(a)) * b
    o_ref[...] += _dot(act.astype(BF16), wo_ref[...])


def _ffn(x, norm_g, w_in, w_out, mix=None, *, tm, fc):
    rows, d = x.shape
    nj = w_out.shape[0] // fc
    row_spec = pl.BlockSpec((tm, d), lambda i, j: (i, 0))
    in_specs = [row_spec]
    args = [x]
    if mix is not None:
        ma, mb, woa, wob = mix
        in_specs += [pl.BlockSpec((tm, ma.shape[1]), lambda i, j: (i, 0)),
                     pl.BlockSpec((tm, mb.shape[1]), lambda i, j: (i, 0)),
                     pl.BlockSpec(woa.shape, lambda i, j: (0, 0)),
                     pl.BlockSpec(wob.shape, lambda i, j: (0, 0))]
        args += [ma, mb, woa, wob]
    in_specs += [pl.BlockSpec((1, d), lambda i, j: (0, 0)),
                 pl.BlockSpec((1, d, fc), lambda i, j: (j, 0, 0)),
                 pl.BlockSpec((1, d, fc), lambda i, j: (j + nj, 0, 0)),
                 pl.BlockSpec((fc, d), lambda i, j: (j, 0))]
    args += [norm_g.reshape(1, d), w_in, w_in, w_out]
    return pl.pallas_call(
        functools.partial(_ffn_kernel, with_mix=mix is not None),
        grid=(rows // tm, nj),
        in_specs=in_specs,
        out_specs=row_spec,
        out_shape=jax.ShapeDtypeStruct((rows, d), F32),
        scratch_shapes=[pltpu.VMEM((tm, d), BF16)],
        compiler_params=_params(2),
        name="ffn_mix" if mix is not None else "ffn",
    )(*args)


def _proj_kernel(x_ref, g_ref, w_ref, qg_ref, kg_ref, lb_ref, hm_ref,
                 kf_ref, vf_ref, qb_ref, kb_ref, vb_ref,
                 qh_ref, kin_ref, lf_ref, hi_ref, gt_ref, *, feature_major):
    h = _rms_rows(x_ref[...], g_ref[...]).astype(BF16)
    w = SB_WIDTH

    def col(c):
        return _dot(h, w_ref[:, c * w:(c + 1) * w])

    def head_norm(a, gain):
        ms = _dot((a * a).astype(BF16), hm_ref[...])
        return a * lax.rsqrt(ms + EPS) * gain

    qn = head_norm(col(0), qg_ref[...])
    qb_ref[...] = (qn * (SB_SCALE * LOG2_E)).astype(BF16)
    kn = head_norm(col(1), kg_ref[...])
    sv = col(2)
    if feature_major:
        kf_ref[0] = kn.T
        vf_ref[0] = sv.T
    else:
        kf_ref[...] = kn
        vf_ref[...] = sv
    kb_ref[...] = kn.astype(BF16)
    vb_ref[...] = sv.astype(BF16)

    hq = col(3)
    qh_ref[...] = hq * _sigmoid(hq)
    hf = col(4)
    lb = lb_ref[...]
    f = lb + (1.0 - lb) * _sigmoid(hf)
    lf_ref[...] = jnp.log2(f)
    kin_ref[...] = (1.0 - lb) * _sigmoid(-hf)
    hi_ref[...] = col(5).astype(BF16)
    hg = col(6)
    gt_ref[...] = hg * _sigmoid(hg)


def _proj(x, norm_g, w_in, q_gain, k_gain, lb, head_mean, *, tm, batch):
    rows, d = x.shape
    t = rows // batch
    w = SB_WIDTH
    feature_major = t % tm == 0
    row_spec = pl.BlockSpec((tm, w), lambda i: (i, 0))
    vec = lambda n: pl.BlockSpec((1, n), lambda i: (0, 0))
    f32o = jax.ShapeDtypeStruct((rows, w), F32)
    bf16o = jax.ShapeDtypeStruct((rows, w), BF16)
    if feature_major:
        nt = t // tm
        kv_spec = pl.BlockSpec((1, w, tm), lambda i: (i // nt, 0, i % nt))
        kv_shape = jax.ShapeDtypeStruct((batch, w, t), F32)
    else:
        kv_spec, kv_shape = row_spec, f32o
    return pl.pallas_call(
        functools.partial(_proj_kernel, feature_major=feature_major),
        grid=(rows // tm,),
        in_specs=[pl.BlockSpec((tm, d), lambda i: (i, 0)), vec(d),
                  pl.BlockSpec(w_in.shape, lambda i: (0, 0)),
                  vec(w), vec(w), vec(w),
                  pl.BlockSpec(head_mean.shape, lambda i: (0, 0))],
        out_specs=[kv_spec, kv_spec] + [row_spec] * 8,
        out_shape=[kv_shape, kv_shape, bf16o, bf16o, bf16o, f32o, f32o, f32o, bf16o, f32o],
        compiler_params=_params(1),
        name="proj",
    )(x, norm_g.reshape(1, d), w_in,
      jnp.tile(q_gain, SB_HEADS).reshape(1, w), jnp.tile(k_gain, SB_HEADS).reshape(1, w),
      lb.reshape(1, w), head_mean)


def _sb_tiles(ops, upper, feature_major=False):
    scores, weighted = (_dot, _dot_nt) if feature_major else (_dot_nt, _dot)
    zs = [scores(q, kblk) for q, kblk, _, _ in ops]
    suffixes = []
    for z, (_, _, _, mask) in zip(zs, ops):
        drop = jnp.maximum(z, jnp.log2(1.0 + jnp.exp2(jnp.minimum(z, EXP2_CLAMP))))
        if mask is not None:
            drop = jnp.where(mask, drop, 0.0)
        suffixes.append(_dot(drop.astype(BF16), upper))
    out = []
    for z, suffix, (_, _, vblk, mask) in zip(zs, suffixes, ops):
        w = jnp.exp2(z - suffix)
        if mask is not None:
            w = jnp.where(mask, w, 0.0)
        out.append((weighted(w.astype(BF16), vblk), suffix[:, :1]))
    return out


def _sb_finish(acc0, acc1, gain):
    lane = lax.broadcasted_iota(jnp.int32, (1, LANES), 1)
    first = lane < SB_HEAD_DIM
    o = jnp.where(first, acc0, acc1)
    sq = o * o
    s0 = jnp.sum(jnp.where(first, sq, 0.0), axis=-1, keepdims=True)
    s1 = jnp.sum(jnp.where(first, 0.0, sq), axis=-1, keepdims=True)
    ms = jnp.where(first, s0, s1) * (1.0 / SB_HEAD_DIM)
    return o * lax.rsqrt(ms + EPS) * gain


def _head_masked(q):
    lane = lax.broadcasted_iota(jnp.int32, (1, LANES), 1)
    first = lane < SB_HEAD_DIM
    zero = jnp.zeros_like(q)
    return jnp.where(first, q, zero), jnp.where(first, zero, q)


def _sb_prompt_kernel(q_ref, k_ref, v_ref, gain_ref, up_ref, o_ref, acc_ref, car_ref, *, tq):
    i = pl.program_id(2)
    qms = _head_masked(q_ref[...])
    upper = up_ref[...]
    row = lax.broadcasted_iota(jnp.int32, (tq, tq), 0)
    colid = lax.broadcasted_iota(jnp.int32, (tq, tq), 1)
    causal = colid < row

    def blocks(kbs, diagonal_first):
        ops = []
        for n, kb in enumerate(kbs):
            start = pl.multiple_of(kb * tq, tq)
            kblk = k_ref[pl.ds(start, tq), :]
            vblk = v_ref[pl.ds(start, tq), :]
            mask = causal if diagonal_first and n == 0 else None
            ops += [(qms[0], kblk, vblk, mask), (qms[1], kblk, vblk, mask)]
        res = _sb_tiles(ops, upper)
        for h in range(2):
            carry = car_ref[h]
            acc = acc_ref[h]
            for pv, total in res[h::2]:
                acc = acc + jnp.exp2(-carry) * pv
                carry = carry + total
            car_ref[h] = carry
            acc_ref[h] = acc

    acc_ref[...] = jnp.zeros_like(acc_ref)
    car_ref[...] = jnp.zeros_like(car_ref)
    for rem in range(SB_BLOCKS_IN_FLIGHT):
        @pl.when(i % SB_BLOCKS_IN_FLIGHT == rem)
        def _(rem=rem):
            blocks([i - n for n in range(rem + 1)], True)

    n_groups = i // SB_BLOCKS_IN_FLIGHT

    def group(j, carry):
        top = SB_BLOCKS_IN_FLIGHT * (n_groups - j) - 1
        blocks([top - n for n in range(SB_BLOCKS_IN_FLIGHT)], False)
        return carry

    lax.fori_loop(0, n_groups, group, 0)
    o_ref[...] = _sb_finish(acc_ref[0], acc_ref[1], gain_ref[...]).astype(o_ref.dtype)


def _sb_prompt(q, k, v, gain_pair, upper, *, batch, tq):
    rows, w = q.shape
    t = rows // batch
    nq = t // tq
    qspec = pl.BlockSpec((tq, LANES), lambda b, p, i: (b * nq + i, p))
    kvspec = pl.BlockSpec((t, LANES), lambda b, p, i: (b, p))
    return pl.pallas_call(
        functools.partial(_sb_prompt_kernel, tq=tq),
        grid=(batch, HEAD_PAIRS, nq),
        in_specs=[qspec, kvspec, kvspec,
                  pl.BlockSpec((1, LANES), lambda b, p, i: (0, 0)),
                  pl.BlockSpec((tq, tq), lambda b, p, i: (0, 0))],
        out_specs=qspec,
        out_shape=jax.ShapeDtypeStruct((rows, w), BF16),
        scratch_shapes=[pltpu.VMEM((2, tq, LANES), F32), pltpu.VMEM((2, tq, 1), F32)],
        compiler_params=_params(3),
        name="sb_prompt",
    )(q, k, v, gain_pair, upper)


def _sb_sample_kernel(q_ref, kn_ref, vn_ref, ck_ref, cv_ref, gain_ref, up_ref, upn_ref,
                      o_ref, *, tk, n_blocks, t_new):
    qms = _head_masked(q_ref[...])
    upper = up_ref[...]
    row = lax.broadcasted_iota(jnp.int32, (t_new, t_new), 0)
    colid = lax.broadcasted_iota(jnp.int32, (t_new, t_new), 1)
    causal = colid < row
    kn = kn_ref[...]
    vn = vn_ref[...]

    res = _sb_tiles([(qm, kn, vn, causal) for qm in qms], upn_ref[...])
    accs = [r[0] for r in res]
    carries = [r[1] for r in res]
    ops = []
    for kb in range(n_blocks - 1, -1, -1):
        keys = slice(kb * tk, (kb + 1) * tk)
        kblk = ck_ref[:, keys].astype(BF16)
        vblk = cv_ref[:, keys].astype(BF16)
        ops += [(qm, kblk, vblk, None) for qm in qms]
    for n, (pv, total) in enumerate(_sb_tiles(ops, upper, feature_major=True)):
        h = n % 2
        accs[h] = accs[h] + jnp.exp2(-carries[h]) * pv
        carries[h] = carries[h] + total
    o_ref[...] = _sb_finish(accs[0], accs[1], gain_ref[...]).astype(o_ref.dtype)


def _sb_sample(q, k_new, v_new, cache_k, cache_v, gain_pair, upper, upper_new, *, batch, tk):
    rows, w = q.shape
    t_new = rows // batch
    past = cache_k.shape[1]
    qspec = pl.BlockSpec((t_new, LANES), lambda b, p: (b, p))
    cspec = pl.BlockSpec((LANES, past), lambda b, p: (b * HEAD_PAIRS + p, 0))
    return pl.pallas_call(
        functools.partial(_sb_sample_kernel, tk=tk, n_blocks=past // tk, t_new=t_new),
        grid=(batch, HEAD_PAIRS),
        in_specs=[qspec, qspec, qspec, cspec, cspec,
                  pl.BlockSpec((1, LANES), lambda b, p: (0, 0)),
                  pl.BlockSpec((tk, tk), lambda b, p: (0, 0)),
                  pl.BlockSpec((t_new, t_new), lambda b, p: (0, 0))],
        out_specs=qspec,
        out_shape=jax.ShapeDtypeStruct((rows, w), BF16),
        compiler_params=_params(2),
        name="sb_sample",
    )(q, k_new, v_new, cache_k, cache_v, gain_pair, upper, upper_new)


def _hgrn_chains(chains, lower, ones, c):
    n = c // SUB
    pieces = []
    for _, _, g, _ in chains:
        g1 = g.astype(BF16)
        r1 = g - g1.astype(F32)
        g2 = r1.astype(BF16)
        g3 = (r1 - g2.astype(F32)).astype(BF16)
        pieces.append(jnp.concatenate([g1, g2, g3], axis=1))
    sums3 = [_dot(lower, p) for p in pieces]

    row = lax.broadcasted_iota(jnp.int32, (c, 1), 0)
    sub = lax.broadcasted_iota(jnp.int32, (1, SUB, 1), 1)
    mids = []
    for (q, k, _, _), bs in zip(chains, sums3):
        b = bs[:, :HG_DIM] + (bs[:, HG_DIM:2 * HG_DIM] + bs[:, 2 * HG_DIM:])
        b_last = b[c - 1:c, :]
        b3 = b.reshape(n, SUB, HG_DIM)
        q3 = q.reshape(n, SUB, HG_DIM)
        k3 = k.reshape(n, SUB, HG_DIM)
        kt = (k3 * jnp.exp2(b3[:, SUB - 1:SUB, :] - b3)).reshape(c, HG_DIM).astype(BF16)
        qs, ks = [], []
        zero = jnp.zeros_like(kt)
        for j in range(n - 1):
            last = SUB * (j + 1)
            qj = q[last:] * jnp.exp2(b[last:] - b[last - 1:last])
            qs.append(jnp.concatenate([jnp.zeros((last, HG_DIM), F32), qj]).astype(BF16))
            ks.append(jnp.where((row >= SUB * j) & (row < last), kt, zero))
        ps = []
        for s in range(SUB):
            e = jnp.exp2(jnp.minimum(b3 - b3[:, s:s + 1, :], 0.0))
            p = q3 * e * k3[:, s:s + 1, :]
            ps.append(jnp.where(sub >= s, p, 0.0).reshape(c, HG_DIM).astype(BF16))
        q0 = (q * jnp.exp2(b)).astype(BF16)
        khat = (k * jnp.exp2(b_last - b)).astype(BF16)
        mids.append((qs, ks, jnp.concatenate(ps, axis=0), q0, khat, jnp.exp2(b_last)))

    far = [_dot_nt(jnp.concatenate(qs, axis=1), jnp.concatenate(ks, axis=1)) if n > 1 else None
           for qs, ks, *_ in mids]
    near = [_dot(m[2], ones) for m in mids]
    colid = lax.broadcasted_iota(jnp.int32, (c, c), 1)
    base = (lax.broadcasted_iota(jnp.int32, (c, c), 0) // SUB) * SUB
    atts = []
    for a, sums in zip(far, near):
        for s in range(SUB):
            d = jnp.where(colid == base + s, sums[s * c:(s + 1) * c, :], 0.0)
            a = d if a is None else a + d
        atts.append(a.astype(BF16))
    avs = [_dot(a, ch[3]) for a, ch in zip(atts, chains)]
    kvs = [_dot_tn(ch[3], m[4]) for ch, m in zip(chains, mids)]
    return [(m[3], av, kv, m[5]) for m, av, kv in zip(mids, avs, kvs)]


def _hgrn_kernel(*refs, c, n_chunks, has_s0):
    if has_s0:
        (qh_ref, kin_ref, lf_ref, hi_ref, gt_ref, gain_ref, low_ref, ones_ref, s0_ref,
         o_ref, s_ref, st_ref) = refs
    else:
        (qh_ref, kin_ref, lf_ref, hi_ref, gt_ref, gain_ref, low_ref, ones_ref,
         o_ref, s_ref, st_ref) = refs
    step = pl.program_id(1)

    @pl.when(step == 0)
    def _():
        for h in range(HG_HEADS):
            st_ref[h] = s0_ref[0, h].T if has_s0 else jnp.zeros((HG_DIM, HG_DIM), F32)

    lower = low_ref[...]
    ones = ones_ref[...]
    gain = gain_ref[...]
    group = min(HGRN_CHUNKS_IN_FLIGHT, n_chunks)

    def chunks(ci, carry):
        where = []
        for u in range(group):
            r0 = pl.multiple_of((ci * group + u) * c, c)
            where += [(pl.ds(r0, c), slice(h * HG_DIM, (h + 1) * HG_DIM))
                      for h in range(HG_HEADS)]
        parts = _hgrn_chains(
            [(qh_ref[rows, ln], kin_ref[rows, ln], lf_ref[rows, ln], hi_ref[rows, ln])
             for rows, ln in where], lower, ones, c)
        for n, ((rows, ln), (q0, av, kv, decay)) in enumerate(zip(where, parts)):
            h = n % HG_HEADS
            st = st_ref[h]
            o = _dot_nt(q0, st.astype(BF16)) + av
            st_ref[h] = st * decay + kv
            y = _rms_rows(o, gain) * gt_ref[rows, ln]
            o_ref[rows, ln] = y.astype(o_ref.dtype)
        return carry

    lax.fori_loop(0, n_chunks // group, chunks, 0)

    @pl.when(step == pl.num_programs(1) - 1)
    def _():
        for h in range(HG_HEADS):
            s_ref[0, h] = st_ref[h].T


def _hgrn(qh, kin, lf, hi, gate, gain, s0, *, batch, c, rb):
    rows, w = qh.shape
    t = rows // batch
    nsteps = t // rb
    row_spec = pl.BlockSpec((rb, w), lambda b, s: (b * nsteps + s, 0))
    lower = jnp.tril(jnp.ones((c, c), BF16))
    ones = jnp.ones((HG_DIM, c), BF16)
    state_spec = pl.BlockSpec((1, HG_HEADS, HG_DIM, HG_DIM), lambda b, s: (b, 0, 0, 0))
    in_specs = [row_spec] * 5 + [pl.BlockSpec((1, HG_DIM), lambda b, s: (0, 0)),
                                 pl.BlockSpec((c, c), lambda b, s: (0, 0)),
                                 pl.BlockSpec((HG_DIM, c), lambda b, s: (0, 0))]
    args = [qh, kin, lf, hi, gate, gain.reshape(1, HG_DIM), lower, ones]
    if s0 is not None:
        in_specs.append(state_spec)
        args.append(s0)
    return pl.pallas_call(
        functools.partial(_hgrn_kernel, c=c, n_chunks=rb // c, has_s0=s0 is not None),
        grid=(batch, nsteps),
        in_specs=in_specs,
        out_specs=[row_spec, state_spec],
        out_shape=[jax.ShapeDtypeStruct((rows, w), BF16),
                   jax.ShapeDtypeStruct((batch, HG_HEADS, HG_DIM, HG_DIM), F32)],
        scratch_shapes=[pltpu.VMEM((HG_HEADS, HG_DIM, HG_DIM), F32)],
        compiler_params=_params(2),
        name="hgrn_s0" if s0 is not None else "hgrn",
    )(*args)


def _tile_sizes(rows, t):
    tm = min(1024, rows)
    tq = min(256, t)
    c = min(64, t)
    rb = min(512, t)
    return tm, tq, c, rb


def _column_chunks(w, fc):
    d, n = w.shape
    return w.reshape(d, n // fc, fc).transpose(1, 0, 2)


def _suffix_ones(n):
    r = lax.broadcasted_iota(jnp.int32, (n, n), 0)
    col = lax.broadcasted_iota(jnp.int32, (n, n), 1)
    return (r >= col).astype(BF16)


def _layer(x, wts, lb, *, batch, cache=None, s0=None):
    rows, _ = x.shape
    t = rows // batch
    tm, tq, c, rb = _tile_sizes(rows, t)
    fc = FFN_CHUNK
    x1 = _ffn(x, wts["ffn1_norm"], wts["ffn1_w_in"], wts["ffn1_w_out"], tm=tm, fc=fc)
    (kf, vf, qb, kb, vb, qh, kin, lf, hi, gate) = _proj(
        x1, wts["mix_norm"], wts["w_in"], wts["sb_q_gain"], wts["sb_k_gain"], lb,
        wts["head_mean"], tm=min(512, rows), batch=batch)
    gain_pair = jnp.tile(wts["sb_out_gain"], LANES // SB_HEAD_DIM).reshape(1, LANES)
    if cache is None:
        sb_o = _sb_prompt(qb, kb, vb, gain_pair, _suffix_ones(tq), batch=batch, tq=tq)
    else:
        tk = min(256, cache[0].shape[1])
        sb_o = _sb_sample(qb, kb, vb, cache[0], cache[1], gain_pair, _suffix_ones(tk),
                          _suffix_ones(t), batch=batch, tk=tk)
    hg_o, state = _hgrn(qh, kin, lf, hi, gate, wts["hg_out_gain"], s0, batch=batch, c=c, rb=rb)
    w_mix = wts["w_out"]
    x3 = _ffn(x1, wts["ffn2_norm"], wts["ffn2_w_in"], wts["ffn2_w_out"],
              mix=(sb_o, hg_o, w_mix[:SB_WIDTH], w_mix[SB_WIDTH:]), tm=tm, fc=fc)
    return x3, kf, vf, state


def _heads(a, batch, t):
    if a.ndim == 3:
        return a.reshape(batch, SB_HEADS, SB_HEAD_DIM, t).transpose(0, 3, 1, 2)
    return a.reshape(batch, t, SB_HEADS, SB_HEAD_DIM)


def kernel(x_prompt, x_sample, cache_sb_k, cache_sb_v, state_hgrn, ffn1_norm, ffn1_w_in,
           ffn1_w_out, mix_norm, w_in, sb_q_gain, sb_k_gain, hg_lb_logits, sb_out_gain,
           hg_out_gain, w_out, ffn2_norm, ffn2_w_in, ffn2_w_out):
    depth = w_in.shape[0]
    bp, tp, d = x_prompt.shape
    bs, ts, _ = x_sample.shape
    past = cache_sb_k.shape[2]
    lb_all = jnp.cumsum(jax.nn.softmax(hg_lb_logits.astype(F32), axis=0), axis=0)
    hid = lax.broadcasted_iota(jnp.int32, (SB_WIDTH, SB_WIDTH), 0) // SB_HEAD_DIM
    hid_t = lax.broadcasted_iota(jnp.int32, (SB_WIDTH, SB_WIDTH), 1) // SB_HEAD_DIM
    head_mean = jnp.where(hid == hid_t, 1.0 / SB_HEAD_DIM, 0.0).astype(BF16)

    xp = x_prompt.reshape(bp * tp, d)
    xs = x_sample.reshape(bs * ts, d)
    outs = [[] for _ in range(6)]
    for l in range(depth):
        wts = dict(
            ffn1_norm=ffn1_norm[l],
            ffn1_w_in=_column_chunks(ffn1_w_in[l].astype(BF16), FFN_CHUNK),
            ffn1_w_out=ffn1_w_out[l].astype(BF16), mix_norm=mix_norm[l],
            w_in=w_in[l].astype(BF16), sb_q_gain=sb_q_gain[l], sb_k_gain=sb_k_gain[l],
            sb_out_gain=sb_out_gain[l], hg_out_gain=hg_out_gain[l],
            w_out=w_out[l].astype(BF16), ffn2_norm=ffn2_norm[l],
            ffn2_w_in=_column_chunks(ffn2_w_in[l].astype(BF16), FFN_CHUNK),
            ffn2_w_out=ffn2_w_out[l].astype(BF16),
            head_mean=head_mean)
        xp, kp, vp, sp = _layer(xp, wts, lb_all[l], batch=bp)
        cache = tuple(c[l].transpose(0, 2, 3, 1).reshape(bs * SB_WIDTH, past)
                      for c in (cache_sb_k, cache_sb_v))
        xs, kn, vn, sn = _layer(xs, wts, lb_all[l], batch=bs, cache=cache, s0=state_hgrn[l])
        for lst, val in zip(outs, (_heads(kp, bp, tp), _heads(vp, bp, tp), sp,
                                   _heads(kn, bs, ts), _heads(vn, bs, ts), sn)):
            lst.append(val)
    return (xp.reshape(bp, tp, d), xs.reshape(bs, ts, d)) + tuple(
        jnp.stack(lst, axis=0) for lst in outs)
```

```python
import functools

import jax
import jax.numpy as jnp
from jax import lax
from jax.experimental import pallas as pl
from jax.experimental.pallas import tpu as pltpu

EPS = 1e-6
SB_HEADS = 8
SB_HEAD_DIM = 64
SB_WIDTH = SB_HEADS * SB_HEAD_DIM
SB_SCALE = SB_HEAD_DIM ** -0.5
LOG2_E = 1.4426950408889634
EXP2_CLAMP = 126.0
HG_HEADS = 4
HG_DIM = 128
HG_WIDTH = HG_HEADS * HG_DIM
LANES = 128
SUB = 8
HEAD_PAIRS = SB_WIDTH // LANES

BF16 = jnp.bfloat16
F32 = jnp.float32

VMEM_LIMIT = 56 * 1024 * 1024
FFN_CHUNK = 256
SB_BLOCKS_IN_FLIGHT = 8
HGRN_CHUNKS_IN_FLIGHT = 4


def _dot(a, b):
    return jnp.dot(a, b, preferred_element_type=F32)


def _dot_nt(a, b):
    return lax.dot_general(a, b, (((1,), (1,)), ((), ())), preferred_element_type=F32)


def _dot_tn(a, b):
    return lax.dot_general(a, b, (((0,), (0,)), ((), ())), preferred_element_type=F32)


def _sigmoid(x):
    return 1.0 / (1.0 + jnp.exp(-x))


def _rms_rows(x, g):
    ms = jnp.mean(x * x, axis=-1, keepdims=True)
    return x * lax.rsqrt(ms + EPS) * g


def _params(n_grid):
    return pltpu.CompilerParams(
        dimension_semantics=("arbitrary",) * n_grid, vmem_limit_bytes=VMEM_LIMIT)


def _ffn_kernel(*refs, with_mix):
    if with_mix:
        (x_ref, ma_ref, mb_ref, woa_ref, wob_ref, g_ref, wa_ref, wb_ref, wo_ref,
         o_ref, h_ref) = refs
    else:
        x_ref, g_ref, wa_ref, wb_ref, wo_ref, o_ref, h_ref = refs

    @pl.when(pl.program_id(1) == 0)
    def _():
        x = x_ref[...]
        if with_mix:
            x = x + _dot(ma_ref[...], woa_ref[...]) + _dot(mb_ref[...], wob_ref[...])
        o_ref[...] = x
        h_ref[...] = _rms_rows(x, g_ref[...]).astype(BF16)

    h = h_ref[...]
    a = _dot(h, wa_ref[0])
    b = _dot(h, wb_ref[0])
    act = a * (0.5 * _sigmoid(a)) * b
    o_ref[...] += _dot(act.astype(BF16), wo_ref[...])


def _ffn(x, norm_g, w_in, w_out, mix=None, *, tm, fc):
    rows, d = x.shape
    nj = w_out.shape[0] // fc
    row_spec = pl.BlockSpec((tm, d), lambda i, j: (i, 0))
    in_specs = [row_spec]
    args = [x]
    if mix is not None:
        ma, mb, woa, wob = mix
        in_specs += [pl.BlockSpec((tm, ma.shape[1]), lambda i, j: (i, 0)),
                     pl.BlockSpec((tm, mb.shape[1]), lambda i, j: (i, 0)),
                     pl.BlockSpec(woa.shape, lambda i, j: (0, 0)),
                     pl.BlockSpec(wob.shape, lambda i, j: (0, 0))]
        args += [ma, mb, woa, wob]
    in_specs += [pl.BlockSpec((1, d), lambda i, j: (0, 0)),
                 pl.BlockSpec((1, d, fc), lambda i, j: (j, 0, 0)),
                 pl.BlockSpec((1, d, fc), lambda i, j: (j + nj, 0, 0)),
                 pl.BlockSpec((fc, d), lambda i, j: (j, 0))]
    args += [norm_g.reshape(1, d), w_in, w_in, w_out]
    return pl.pallas_call(
        functools.partial(_ffn_kernel, with_mix=mix is not None),
        grid=(rows // tm, nj),
        in_specs=in_specs,
        out_specs=row_spec,
        out_shape=jax.ShapeDtypeStruct((rows, d), F32),
        scratch_shapes=[pltpu.VMEM((tm, d), BF16)],
        compiler_params=_params(2),
        name="ffn_mix" if mix is not None else "ffn",
    )(*args)


def _proj_kernel(x_ref, g_ref, w_ref, qg_ref, kg_ref, lb_ref, hm_ref,
                 kf_ref, vf_ref, qb_ref, kb_ref, vb_ref,
                 qh_ref, kin_ref, lf_ref, hi_ref, gt_ref, *, feature_major):
    h = _rms_rows(x_ref[...], g_ref[...]).astype(BF16)
    w = SB_WIDTH

    def col(c):
        return _dot(h, w_ref[:, c * w:(c + 1) * w])

    def head_norm(a, gain):
        ms = _dot((a * a).astype(BF16), hm_ref[...])
        return a * lax.rsqrt(ms + EPS) * gain

    qn = head_norm(col(0), qg_ref[...])
    qb_ref[...] = (qn * (SB_SCALE * LOG2_E)).astype(BF16)
    kn = head_norm(col(1), kg_ref[...])
    sv = col(2)
    if feature_major:
        kf_ref[0] = kn.T
        vf_ref[0] = sv.T
    else:
        kf_ref[...] = kn
        vf_ref[...] = sv
    kb_ref[...] = kn.astype(BF16)
    vb_ref[...] = sv.astype(BF16)

    hq = col(3)
    qh_ref[...] = hq * _sigmoid(hq)
    hf = col(4)
    lb = lb_ref[...]
    f = lb + (1.0 - lb) * _sigmoid(hf)
    lf_ref[...] = jnp.log2(f)
    kin_ref[...] = (1.0 - lb) * _sigmoid(-hf)
    hi_ref[...] = col(5).astype(BF16)
    hg = col(6)
    gt_ref[...] = hg * _sigmoid(hg)


def _proj(x, norm_g, w_in, q_gain, k_gain, lb, head_mean, *, tm, batch):
    rows, d = x.shape
    t = rows // batch
    w = SB_WIDTH
    feature_major = t % tm == 0
    row_spec = pl.BlockSpec((tm, w), lambda i: (i, 0))
    vec = lambda n: pl.BlockSpec((1, n), lambda i: (0, 0))
    f32o = jax.ShapeDtypeStruct((rows, w), F32)
    bf16o = jax.ShapeDtypeStruct((rows, w), BF16)
    if feature_major:
        nt = t // tm
        kv_spec = pl.BlockSpec((1, w, tm), lambda i: (i // nt, 0, i % nt))
        kv_shape = jax.ShapeDtypeStruct((batch, w, t), F32)
    else:
        kv_spec, kv_shape = row_spec, f32o
    return pl.pallas_call(
        functools.partial(_proj_kernel, feature_major=feature_major),
        grid=(rows // tm,),
        in_specs=[pl.BlockSpec((tm, d), lambda i: (i, 0)), vec(d),
                  pl.BlockSpec(w_in.shape, lambda i: (0, 0)),
                  vec(w), vec(w), vec(w),
                  pl.BlockSpec(head_mean.shape, lambda i: (0, 0))],
        out_specs=[kv_spec, kv_spec] + [row_spec] * 8,
        out_shape=[kv_shape, kv_shape, bf16o, bf16o, bf16o, f32o, f32o, f32o, bf16o, f32o],
        compiler_params=_params(1),
        name="proj",
    )(x, norm_g.reshape(1, d), w_in,
      jnp.tile(q_gain, SB_HEADS).reshape(1, w), jnp.tile(k_gain, SB_HEADS).reshape(1, w),
      lb.reshape(1, w), head_mean)


def _sb_tiles(ops, upper, feature_major=False):
    scores, weighted = (_dot, _dot_nt) if feature_major else (_dot_nt, _dot)
    zs = [scores(q, kblk) for q, kblk, _, _ in ops]
    suffixes = []
    for z, (_, _, _, mask) in zip(zs, ops):
        drop = jnp.maximum(z, jnp.log2(1.0 + jnp.exp2(jnp.minimum(z, EXP2_CLAMP))))
        if mask is not None:
            drop = jnp.where(mask, drop, 0.0)
        suffixes.append(_dot(drop.astype(BF16), upper))
    out = []
    for z, suffix, (_, _, vblk, mask) in zip(zs, suffixes, ops):
        w = jnp.exp2(z - suffix)
        if mask is not None:
            w = jnp.where(mask, w, 0.0)
        out.append((weighted(w.astype(BF16), vblk), suffix[:, :1]))
    return out


def _sb_finish(acc0, acc1, gain):
    lane = lax.broadcasted_iota(jnp.int32, (1, LANES), 1)
    first = lane < SB_HEAD_DIM
    o = jnp.where(first, acc0, acc1)
    sq = o * o
    s0 = jnp.sum(jnp.where(first, sq, 0.0), axis=-1, keepdims=True)
    s1 = jnp.sum(jnp.where(first, 0.0, sq), axis=-1, keepdims=True)
    ms = jnp.where(first, s0, s1) * (1.0 / SB_HEAD_DIM)
    return o * lax.rsqrt(ms + EPS) * gain


def _head_masked(q):
    lane = lax.broadcasted_iota(jnp.int32, (1, LANES), 1)
    first = lane < SB_HEAD_DIM
    zero = jnp.zeros_like(q)
    return jnp.where(first, q, zero), jnp.where(first, zero, q)


def _sb_prompt_kernel(q_ref, k_ref, v_ref, gain_ref, up_ref, o_ref, acc_ref, car_ref, *, tq):
    i = pl.program_id(2)
    qms = _head_masked(q_ref[...])
    upper = up_ref[...]
    row = lax.broadcasted_iota(jnp.int32, (tq, tq), 0)
    colid = lax.broadcasted_iota(jnp.int32, (tq, tq), 1)
    causal = colid < row

    def blocks(kbs, diagonal_first):
        ops = []
        for n, kb in enumerate(kbs):
            start = pl.multiple_of(kb * tq, tq)
            kblk = k_ref[pl.ds(start, tq), :]
            vblk = v_ref[pl.ds(start, tq), :]
            mask = causal if diagonal_first and n == 0 else None
            ops += [(qms[0], kblk, vblk, mask), (qms[1], kblk, vblk, mask)]
        res = _sb_tiles(ops, upper)
        for h in range(2):
            carry = car_ref[h]
            acc = acc_ref[h]
            for pv, total in res[h::2]:
                acc = acc + jnp.exp2(-carry) * pv
                carry = carry + total
            car_ref[h] = carry
            acc_ref[h] = acc

    acc_ref[...] = jnp.zeros_like(acc_ref)
    car_ref[...] = jnp.zeros_like(car_ref)
    for rem in range(SB_BLOCKS_IN_FLIGHT):
        @pl.when(i % SB_BLOCKS_IN_FLIGHT == rem)
        def _(rem=rem):
            blocks([i - n for n in range(rem + 1)], True)

    n_groups = i // SB_BLOCKS_IN_FLIGHT

    def group(j, carry):
        top = SB_BLOCKS_IN_FLIGHT * (n_groups - j) - 1
        blocks([top - n for n in range(SB_BLOCKS_IN_FLIGHT)], False)
        return carry

    lax.fori_loop(0, n_groups, group, 0)
    o_ref[...] = _sb_finish(acc_ref[0], acc_ref[1], gain_ref[...]).astype(o_ref.dtype)


def _sb_prompt(q, k, v, gain_pair, upper, *, batch, tq):
    rows, w = q.shape
    t = rows // batch
    nq = t // tq
    qspec = pl.BlockSpec((tq, LANES), lambda b, p, i: (b * nq + i, p))
    kvspec = pl.BlockSpec((t, LANES), lambda b, p, i: (b, p))
    return pl.pallas_call(
        functools.partial(_sb_prompt_kernel, tq=tq),
        grid=(batch, HEAD_PAIRS, nq),
        in_specs=[qspec, kvspec, kvspec,
                  pl.BlockSpec((1, LANES), lambda b, p, i: (0, 0)),
                  pl.BlockSpec((tq, tq), lambda b, p, i: (0, 0))],
        out_specs=qspec,
        out_shape=jax.ShapeDtypeStruct((rows, w), BF16),
        scratch_shapes=[pltpu.VMEM((2, tq, LANES), F32), pltpu.VMEM((2, tq, 1), F32)],
        compiler_params=_params(3),
        name="sb_prompt",
    )(q, k, v, gain_pair, upper)


def _sb_sample_kernel(q_ref, kn_ref, vn_ref, ck_ref, cv_ref, gain_ref, up_ref, upn_ref,
                      o_ref, *, tk, n_blocks, t_new):
    qms = _head_masked(q_ref[...])
    upper = up_ref[...]
    row = lax.broadcasted_iota(jnp.int32, (t_new, t_new), 0)
    colid = lax.broadcasted_iota(jnp.int32, (t_new, t_new), 1)
    causal = colid < row
    kn = kn_ref[...]
    vn = vn_ref[...]

    res = _sb_tiles([(qm, kn, vn, causal) for qm in qms], upn_ref[...])
    accs = [r[0] for r in res]
    carries = [r[1] for r in res]
    ops = []
    for kb in range(n_blocks - 1, -1, -1):
        keys = slice(kb * tk, (kb + 1) * tk)
        kblk = ck_ref[:, keys].astype(BF16)
        vblk = cv_ref[:, keys].astype(BF16)
        ops += [(qm, kblk, vblk, None) for qm in qms]
    for n, (pv, total) in enumerate(_sb_tiles(ops, upper, feature_major=True)):
        h = n % 2
        accs[h] = accs[h] + jnp.exp2(-carries[h]) * pv
        carries[h] = carries[h] + total
    o_ref[...] = _sb_finish(accs[0], accs[1], gain_ref[...]).astype(o_ref.dtype)


def _sb_sample(q, k_new, v_new, cache_k, cache_v, gain_pair, upper, upper_new, *, batch, tk):
    rows, w = q.shape
    t_new = rows // batch
    past = cache_k.shape[1]
    qspec = pl.BlockSpec((t_new, LANES), lambda b, p: (b, p))
    cspec = pl.BlockSpec((LANES, past), lambda b, p: (b * HEAD_PAIRS + p, 0))
    return pl.pallas_call(
        functools.partial(_sb_sample_kernel, tk=tk, n_blocks=past // tk, t_new=t_new),
        grid=(batch, HEAD_PAIRS),
        in_specs=[qspec, qspec, qspec, cspec, cspec,
                  pl.BlockSpec((1, LANES), lambda b, p: (0, 0)),
                  pl.BlockSpec((tk, tk), lambda b, p: (0, 0)),
                  pl.BlockSpec((t_new, t_new), lambda b, p: (0, 0))],
        out_specs=qspec,
        out_shape=jax.ShapeDtypeStruct((rows, w), BF16),
        compiler_params=_params(2),
        name="sb_sample",
    )(q, k_new, v_new, cache_k, cache_v, gain_pair, upper, upper_new)


def _hgrn_chains(chains, lower, ones, c):
    n = c // SUB
    pieces = []
    for _, _, g, _ in chains:
        g1 = g.astype(BF16)
        r1 = g - g1.astype(F32)
        g2 = r1.astype(BF16)
        g3 = (r1 - g2.astype(F32)).astype(BF16)
        pieces.append(jnp.concatenate([g1, g2, g3], axis=1))
    sums3 = [_dot(lower, p) for p in pieces]

    row = lax.broadcasted_iota(jnp.int32, (c, 1), 0)
    sub = lax.broadcasted_iota(jnp.int32, (1, SUB, 1), 1)
    mids = []
    for (q, k, _, _), bs in zip(chains, sums3):
        b = bs[:, :HG_DIM] + (bs[:, HG_DIM:2 * HG_DIM] + bs[:, 2 * HG_DIM:])
        b_last = b[c - 1:c, :]
        b3 = b.reshape(n, SUB, HG_DIM)
        q3 = q.reshape(n, SUB, HG_DIM)
        k3 = k.reshape(n, SUB, HG_DIM)
        kt = (k3 * jnp.exp2(b3[:, SUB - 1:SUB, :] - b3)).reshape(c, HG_DIM).astype(BF16)
        qs, ks = [], []
        zero = jnp.zeros_like(kt)
        for j in range(n - 1):
            last = SUB * (j + 1)
            qj = q[last:] * jnp.exp2(b[last:] - b[last - 1:last])
            qs.append(jnp.concatenate([jnp.zeros((last, HG_DIM), F32), qj]).astype(BF16))
            ks.append(jnp.where((row >= SUB * j) & (row < last), kt, zero))
        ps = []
        for s in range(SUB):
            e = jnp.exp2(jnp.minimum(b3 - b3[:, s:s + 1, :], 0.0))
            p = q3 * e * k3[:, s:s + 1, :]
            ps.append(jnp.where(sub >= s, p, 0.0).reshape(c, HG_DIM).astype(BF16))
        q0 = (q * jnp.exp2(b)).astype(BF16)
        khat = (k * jnp.exp2(b_last - b)).astype(BF16)
        mids.append((qs, ks, jnp.concatenate(ps, axis=0), q0, khat, jnp.exp2(b_last)))

    far = [_dot_nt(jnp.concatenate(qs, axis=1), jnp.concatenate(ks, axis=1)) if n > 1 else None
           for qs, ks, *_ in mids]
    near = [_dot(m[2], ones) for m in mids]
    colid = lax.broadcasted_iota(jnp.int32, (c, c), 1)
    base = (lax.broadcasted_iota(jnp.int32, (c, c), 0) // SUB) * SUB
    atts = []
    for a, sums in zip(far, near):
        for s in range(SUB):
            d = jnp.where(colid == base + s, sums[s * c:(s + 1) * c, :], 0.0)
            a = d if a is None else a + d
        atts.append(a.astype(BF16))
    avs = [_dot(a, ch[3]) for a, ch in zip(atts, chains)]
    kvs = [_dot_tn(ch[3], m[4]) for ch, m in zip(chains, mids)]
    return [(m[3], av, kv, m[5]) for m, av, kv in zip(mids, avs, kvs)]


def _hgrn_kernel(*refs, c, n_chunks, has_s0):
    if has_s0:
        (qh_ref, kin_ref, lf_ref, hi_ref, gt_ref, gain_ref, low_ref, ones_ref, s0_ref,
         o_ref, s_ref, st_ref) = refs
    else:
        (qh_ref, kin_ref, lf_ref, hi_ref, gt_ref, gain_ref, low_ref, ones_ref,
         o_ref, s_ref, st_ref) = refs
    step = pl.program_id(1)

    @pl.when(step == 0)
    def _():
        for h in range(HG_HEADS):
            st_ref[h] = s0_ref[0, h].T if has_s0 else jnp.zeros((HG_DIM, HG_DIM), F32)

    lower = low_ref[...]
    ones = ones_ref[...]
    gain = gain_ref[...]
    group = min(HGRN_CHUNKS_IN_FLIGHT, n_chunks)

    def chunks(ci, carry):
        where = []
        for u in range(group):
            r0 = pl.multiple_of((ci * group + u) * c, c)
            where += [(pl.ds(r0, c), slice(h * HG_DIM, (h + 1) * HG_DIM))
                      for h in range(HG_HEADS)]
        parts = _hgrn_chains(
            [(qh_ref[rows, ln], kin_ref[rows, ln], lf_ref[rows, ln], hi_ref[rows, ln])
             for rows, ln in where], lower, ones, c)
        for n, ((rows, ln), (q0, av, kv, decay)) in enumerate(zip(where, parts)):
            h = n % HG_HEADS
            st = st_ref[h]
            o = _dot_nt(q0, st.astype(BF16)) + av
            st_ref[h] = st * decay + kv
            y = _rms_rows(o, gain) * gt_ref[rows, ln]
            o_ref[rows, ln] = y.astype(o_ref.dtype)
        return carry

    lax.fori_loop(0, n_chunks // group, chunks, 0)

    @pl.when(step == pl.num_programs(1) - 1)
    def _():
        for h in range(HG_HEADS):
            s_ref[0, h] = st_ref[h].T


def _hgrn(qh, kin, lf, hi, gate, gain, s0, *, batch, c, rb):
    rows, w = qh.shape
    t = rows // batch
    nsteps = t // rb
    row_spec = pl.BlockSpec((rb, w), lambda b, s: (b * nsteps + s, 0))
    lower = jnp.tril(jnp.ones((c, c), BF16))
    ones = jnp.ones((HG_DIM, c), BF16)
    state_spec = pl.BlockSpec((1, HG_HEADS, HG_DIM, HG_DIM), lambda b, s: (b, 0, 0, 0))
    in_specs = [row_spec] * 5 + [pl.BlockSpec((1, HG_DIM), lambda b, s: (0, 0)),
                                 pl.BlockSpec((c, c), lambda b, s: (0, 0)),
                                 pl.BlockSpec((HG_DIM, c), lambda b, s: (0, 0))]
    args = [qh, kin, lf, hi, gate, gain.reshape(1, HG_DIM), lower, ones]
    if s0 is not None:
        in_specs.append(state_spec)
        args.append(s0)
    return pl.pallas_call(
        functools.partial(_hgrn_kernel, c=c, n_chunks=rb // c, has_s0=s0 is not None),
        grid=(batch, nsteps),
        in_specs=in_specs,
        out_specs=[row_spec, state_spec],
        out_shape=[jax.ShapeDtypeStruct((rows, w), BF16),
                   jax.ShapeDtypeStruct((batch, HG_HEADS, HG_DIM, HG_DIM), F32)],
        scratch_shapes=[pltpu.VMEM((HG_HEADS, HG_DIM, HG_DIM), F32)],
        compiler_params=_params(2),
        name="hgrn_s0" if s0 is not None else "hgrn",
    )(*args)


def _tile_sizes(rows, t):
    tm = min(1024, rows)
    tq = min(256, t)
    c = min(64, t)
    rb = min(512, t)
    return tm, tq, c, rb


def _column_chunks(w, fc):
    d, n = w.shape
    return w.reshape(d, n // fc, fc).transpose(1, 0, 2)


def _suffix_ones(n):
    r = lax.broadcasted_iota(jnp.int32, (n, n), 0)
    col = lax.broadcasted_iota(jnp.int32, (n, n), 1)
    return (r >= col).astype(BF16)


def _layer(x, wts, lb, *, batch, cache=None, s0=None):
    rows, _ = x.shape
    t = rows // batch
    tm, tq, c, rb = _tile_sizes(rows, t)
    fc = FFN_CHUNK
    x1 = _ffn(x, wts["ffn1_norm"], wts["ffn1_w_in"], wts["ffn1_w_out"], tm=tm, fc=fc)
    (kf, vf, qb, kb, vb, qh, kin, lf, hi, gate) = _proj(
        x1, wts["mix_norm"], wts["w_in"], wts["sb_q_gain"], wts["sb_k_gain"], lb,
        wts["head_mean"], tm=min(512, rows), batch=batch)
    gain_pair = jnp.tile(wts["sb_out_gain"], LANES // SB_HEAD_DIM).reshape(1, LANES)
    if cache is None:
        sb_o = _sb_prompt(qb, kb, vb, gain_pair, _suffix_ones(tq), batch=batch, tq=tq)
    else:
        tk = min(256, cache[0].shape[1])
        sb_o = _sb_sample(qb, kb, vb, cache[0], cache[1], gain_pair, _suffix_ones(tk),
                          _suffix_ones(t), batch=batch, tk=tk)
    hg_o, state = _hgrn(qh, kin, lf, hi, gate, wts["hg_out_gain"], s0, batch=batch, c=c, rb=rb)
    w_mix = wts["w_out"]
    x3 = _ffn(x1, wts["ffn2_norm"], wts["ffn2_w_in"], wts["ffn2_w_out"],
              mix=(sb_o, hg_o, w_mix[:SB_WIDTH], w_mix[SB_WIDTH:]), tm=tm, fc=fc)
    return x3, kf, vf, state


def _heads(a, batch, t):
    if a.ndim == 3:
        return a.reshape(batch, SB_HEADS, SB_HEAD_DIM, t).transpose(0, 3, 1, 2)
    return a.reshape(batch, t, SB_HEADS, SB_HEAD_DIM)


def kernel(x_prompt, x_sample, cache_sb_k, cache_sb_v, state_hgrn, ffn1_norm, ffn1_w_in,
           ffn1_w_out, mix_norm, w_in, sb_q_gain, sb_k_gain, hg_lb_logits, sb_out_gain,
           hg_out_gain, w_out, ffn2_norm, ffn2_w_in, ffn2_w_out):
    depth = w_in.shape[0]
    bp, tp, d = x_prompt.shape
    bs, ts, _ = x_sample.shape
    past = cache_sb_k.shape[2]
    lb_all = jnp.cumsum(jax.nn.softmax(hg_lb_logits.astype(F32), axis=0), axis=0)
    hid = lax.broadcasted_iota(jnp.int32, (SB_WIDTH, SB_WIDTH), 0) // SB_HEAD_DIM
    hid_t = lax.broadcasted_iota(jnp.int32, (SB_WIDTH, SB_WIDTH), 1) // SB_HEAD_DIM
    head_mean = jnp.where(hid == hid_t, 1.0 / SB_HEAD_DIM, 0.0).astype(BF16)

    xp = x_prompt.reshape(bp * tp, d)
    xs = x_sample.reshape(bs * ts, d)
    outs = [[] for _ in range(6)]
    for l in range(depth):
        wts = dict(
            ffn1_norm=ffn1_norm[l],
            ffn1_w_in=_column_chunks(ffn1_w_in[l].astype(BF16), FFN_CHUNK),
            ffn1_w_out=ffn1_w_out[l].astype(BF16), mix_norm=mix_norm[l],
            w_in=w_in[l].astype(BF16), sb_q_gain=sb_q_gain[l], sb_k_gain=sb_k_gain[l],
            sb_out_gain=sb_out_gain[l], hg_out_gain=hg_out_gain[l],
            w_out=w_out[l].astype(BF16), ffn2_norm=ffn2_norm[l],
            ffn2_w_in=_column_chunks(ffn2_w_in[l].astype(BF16), FFN_CHUNK),
            ffn2_w_out=ffn2_w_out[l].astype(BF16),
            head_mean=head_mean)
        xp, kp, vp, sp = _layer(xp, wts, lb_all[l], batch=bp)
        cache = tuple(c[l].transpose(0, 2, 3, 1).reshape(bs * SB_WIDTH, past)
                      for c in (cache_sb_k, cache_sb_v))
        xs, kn, vn, sn = _layer(xs, wts, lb_all[l], batch=bs, cache=cache, s0=state_hgrn[l])
        for lst, val in zip(outs, (_heads(kp, bp, tp), _heads(vp, bp, tp), sp,
                                   _heads(kn, bs, ts), _heads(vn, bs, ts), sn)):
            lst.append(val)
    return (xp.reshape(bp, tp, d), xs.reshape(bs, ts, d)) + tuple(
        jnp.stack(lst, axis=0) for lst in outs)
```

```python
import functools

import jax
import jax.numpy as jnp
from jax import lax
from jax.experimental import pallas as pl
from jax.experimental.pallas import tpu as pltpu

EPS = 1e-6
SB_HEADS = 8
SB_HEAD_DIM = 64
SB_WIDTH = SB_HEADS * SB_HEAD_DIM
SB_SCALE = SB_HEAD_DIM ** -0.5
LOG2_E = 1.4426950408889634
EXP2_CLAMP = 126.0
HG_HEADS = 4
HG_DIM = 128
HG_WIDTH = HG_HEADS * HG_DIM
LANES = 128
SUB = 8
HEAD_PAIRS = SB_WIDTH // LANES

BF16 = jnp.bfloat16
F32 = jnp.float32

VMEM_LIMIT = 56 * 1024 * 1024
FFN_CHUNK = 256
SB_BLOCKS_IN_FLIGHT = 8
HGRN_CHUNKS_IN_FLIGHT = 4


def _dot(a, b):
    return jnp.dot(a, b, preferred_element_type=F32)


def _dot_nt(a, b):
    return lax.dot_general(a, b, (((1,), (1,)), ((), ())), preferred_element_type=F32)


def _dot_tn(a, b):
    return lax.dot_general(a, b, (((0,), (0,)), ((), ())), preferred_element_type=F32)


def _sigmoid(x):
    return 1.0 / (1.0 + jnp.exp(-x))


def _rms_rows(x, g):
    ms = jnp.mean(x * x, axis=-1, keepdims=True)
    return x * lax.rsqrt(ms + EPS) * g


def _params(n_grid):
    return pltpu.CompilerParams(
        dimension_semantics=("arbitrary",) * n_grid, vmem_limit_bytes=VMEM_LIMIT)


def _ffn_kernel(*refs, with_mix):
    if with_mix:
        (x_ref, ma_ref, mb_ref, woa_ref, wob_ref, g_ref, wa_ref, wb_ref, wo_ref,
         o_ref, h_ref) = refs
    else:
        x_ref, g_ref, wa_ref, wb_ref, wo_ref, o_ref, h_ref = refs

    @pl.when(pl.program_id(1) == 0)
    def _():
        x = x_ref[...]
        if with_mix:
            x = x + _dot(ma_ref[...], woa_ref[...]) + _dot(mb_ref[...], wob_ref[...])
        o_ref[...] = x
        h_ref[...] = _rms_rows(x, g_ref[...]).astype(BF16)

    h = h_ref[...]
    a = _dot(h, wa_ref[0])
    b = _dot(h, wb_ref[0])
    act = a * (0.5 * _sigmoid(a)) * b
    o_ref[...] += _dot(act.astype(BF16), wo_ref[...])


def _ffn(x, norm_g, w_in, w_out, mix=None, *, tm, fc):
    rows, d = x.shape
    nj = w_out.shape[0] // fc
    row_spec = pl.BlockSpec((tm, d), lambda i, j: (i, 0))
    in_specs = [row_spec]
    args = [x]
    if mix is not None:
        ma, mb, woa, wob = mix
        in_specs += [pl.BlockSpec((tm, ma.shape[1]), lambda i, j: (i, 0)),
                     pl.BlockSpec((tm, mb.shape[1]), lambda i, j: (i, 0)),
                     pl.BlockSpec(woa.shape, lambda i, j: (0, 0)),
                     pl.BlockSpec(wob.shape, lambda i, j: (0, 0))]
        args += [ma, mb, woa, wob]
    in_specs += [pl.BlockSpec((1, d), lambda i, j: (0, 0)),
                 pl.BlockSpec((1, d, fc), lambda i, j: (j, 0, 0)),
                 pl.BlockSpec((1, d, fc), lambda i, j: (j + nj, 0, 0)),
                 pl.BlockSpec((fc, d), lambda i, j: (j, 0))]
    args += [norm_g.reshape(1, d), w_in, w_in, w_out]
    return pl.pallas_call(
        functools.partial(_ffn_kernel, with_mix=mix is not None),
        grid=(rows // tm, nj),
        in_specs=in_specs,
        out_specs=row_spec,
        out_shape=jax.ShapeDtypeStruct((rows, d), F32),
        scratch_shapes=[pltpu.VMEM((tm, d), BF16)],
        compiler_params=_params(2),
        name="ffn_mix" if mix is not None else "ffn",
    )(*args)


def _proj_kernel(x_ref, g_ref, w_ref, qg_ref, kg_ref, lb_ref, hm_ref,
                 kf_ref, vf_ref, qb_ref, kb_ref, vb_ref,
                 qh_ref, kin_ref, lf_ref, hi_ref, gt_ref, *, feature_major):
    h = _rms_rows(x_ref[...], g_ref[...]).astype(BF16)
    w = SB_WIDTH

    def col(c):
        return _dot(h, w_ref[:, c * w:(c + 1) * w])

    def head_norm(a, gain):
        ms = _dot((a * a).astype(BF16), hm_ref[...])
        return a * lax.rsqrt(ms + EPS) * gain

    qn = head_norm(col(0), qg_ref[...])
    qb_ref[...] = (qn * (SB_SCALE * LOG2_E)).astype(BF16)
    kn = head_norm(col(1), kg_ref[...])
    sv = col(2)
    if feature_major:
        kf_ref[0] = kn.T
        vf_ref[0] = sv.T
    else:
        kf_ref[...] = kn
        vf_ref[...] = sv
    kb_ref[...] = kn.astype(BF16)
    vb_ref[...] = sv.astype(BF16)

    hq = col(3)
    qh_ref[...] = hq * _sigmoid(hq)
    hf = col(4)
    lb = lb_ref[...]
    f = lb + (1.0 - lb) * _sigmoid(hf)
    lf_ref[...] = jnp.log2(f)
    kin_ref[...] = (1.0 - lb) * _sigmoid(-hf)
    hi_ref[...] = col(5).astype(BF16)
    hg = col(6)
    gt_ref[...] = hg * _sigmoid(hg)


def _proj(x, norm_g, w_in, q_gain, k_gain, lb, head_mean, *, tm, batch):
    rows, d = x.shape
    t = rows // batch
    w = SB_WIDTH
    feature_major = t % tm == 0
    row_spec = pl.BlockSpec((tm, w), lambda i: (i, 0))
    vec = lambda n: pl.BlockSpec((1, n), lambda i: (0, 0))
    f32o = jax.ShapeDtypeStruct((rows, w), F32)
    bf16o = jax.ShapeDtypeStruct((rows, w), BF16)
    if feature_major:
        nt = t // tm
        kv_spec = pl.BlockSpec((1, w, tm), lambda i: (i // nt, 0, i % nt))
        kv_shape = jax.ShapeDtypeStruct((batch, w, t), F32)
    else:
        kv_spec, kv_shape = row_spec, f32o
    return pl.pallas_call(
        functools.partial(_proj_kernel, feature_major=feature_major),
        grid=(rows // tm,),
        in_specs=[pl.BlockSpec((tm, d), lambda i: (i, 0)), vec(d),
                  pl.BlockSpec(w_in.shape, lambda i: (0, 0)),
                  vec(w), vec(w), vec(w),
                  pl.BlockSpec(head_mean.shape, lambda i: (0, 0))],
        out_specs=[kv_spec, kv_spec] + [row_spec] * 8,
        out_shape=[kv_shape, kv_shape, bf16o, bf16o, bf16o, f32o, f32o, f32o, bf16o, f32o],
        compiler_params=_params(1),
        name="proj",
    )(x, norm_g.reshape(1, d), w_in,
      jnp.tile(q_gain, SB_HEADS).reshape(1, w), jnp.tile(k_gain, SB_HEADS).reshape(1, w),
      lb.reshape(1, w), head_mean)


def _sb_tiles(qs, k_blocks, v_slab, masks, upper, carries, feature_major=False):
    scores, weighted = (_dot, _dot_nt) if feature_major else (_dot_nt, _dot)
    tiles = [(h, n) for n in range(len(k_blocks)) for h in range(len(qs))]
    zs = [scores(qs[h], k_blocks[n]) for h, n in tiles]
    suffixes = []
    for z, (_, n) in zip(zs, tiles):
        drop = jnp.maximum(z, jnp.log2(1.0 + jnp.exp2(jnp.minimum(z, EXP2_CLAMP))))
        if masks[n] is not None:
            drop = jnp.where(masks[n], drop, 0.0)
        suffixes.append(_dot(drop.astype(BF16), upper))
    carries = list(carries)
    weights = [[] for _ in qs]
    for z, suffix, (h, n) in zip(zs, suffixes, tiles):
        w = jnp.exp2(z - suffix - carries[h])
        if masks[n] is not None:
            w = jnp.where(masks[n], w, 0.0)
        weights[h].insert(0, w.astype(BF16))
        carries[h] = carries[h] + suffix[:, :1]
    return [weighted(jnp.concatenate(ws, axis=1), v_slab) for ws in weights], carries


def _sb_finish(acc0, acc1, gain):
    lane = lax.broadcasted_iota(jnp.int32, (1, LANES), 1)
    first = lane < SB_HEAD_DIM
    o = jnp.where(first, acc0, acc1)
    sq = o * o
    s0 = jnp.sum(jnp.where(first, sq, 0.0), axis=-1, keepdims=True)
    s1 = jnp.sum(jnp.where(first, 0.0, sq), axis=-1, keepdims=True)
    ms = jnp.where(first, s0, s1) * (1.0 / SB_HEAD_DIM)
    return o * lax.rsqrt(ms + EPS) * gain


def _head_masked(q):
    lane = lax.broadcasted_iota(jnp.int32, (1, LANES), 1)
    first = lane < SB_HEAD_DIM
    zero = jnp.zeros_like(q)
    return jnp.where(first, q, zero), jnp.where(first, zero, q)


def _sb_prompt_kernel(q_ref, k_ref, v_ref, gain_ref, up_ref, o_ref, acc_ref, car_ref, *, tq):
    i = pl.program_id(2)
    qms = _head_masked(q_ref[...])
    upper = up_ref[...]
    row = lax.broadcasted_iota(jnp.int32, (tq, tq), 0)
    colid = lax.broadcasted_iota(jnp.int32, (tq, tq), 1)
    causal = colid < row

    def blocks(top, count, diagonal_first):
        k_blocks = [k_ref[pl.ds(pl.multiple_of((top - n) * tq, tq), tq), :] for n in range(count)]
        oldest = pl.multiple_of((top - count + 1) * tq, tq)
        masks = [causal if diagonal_first and n == 0 else None for n in range(count)]
        res, carries = _sb_tiles(qms, k_blocks, v_ref[pl.ds(oldest, count * tq), :], masks,
                                 upper, [car_ref[0], car_ref[1]])
        for h in range(2):
            car_ref[h] = carries[h]
            acc_ref[h] += res[h]

    acc_ref[...] = jnp.zeros_like(acc_ref)
    car_ref[...] = jnp.zeros_like(car_ref)
    for rem in range(SB_BLOCKS_IN_FLIGHT):
        @pl.when(i % SB_BLOCKS_IN_FLIGHT == rem)
        def _(rem=rem):
            blocks(i, rem + 1, True)

    n_groups = i // SB_BLOCKS_IN_FLIGHT

    def group(j, carry):
        top = SB_BLOCKS_IN_FLIGHT * (n_groups - j) - 1
        blocks(top, SB_BLOCKS_IN_FLIGHT, False)
        return carry

    lax.fori_loop(0, n_groups, group, 0)
    o_ref[...] = _sb_finish(acc_ref[0], acc_ref[1], gain_ref[...]).astype(o_ref.dtype)


def _sb_prompt(q, k, v, gain_pair, upper, *, batch, tq):
    rows, w = q.shape
    t = rows // batch
    nq = t // tq
    qspec = pl.BlockSpec((tq, LANES), lambda b, p, i: (b * nq + i, p))
    kvspec = pl.BlockSpec((t, LANES), lambda b, p, i: (b, p))
    return pl.pallas_call(
        functools.partial(_sb_prompt_kernel, tq=tq),
        grid=(batch, HEAD_PAIRS, nq),
        in_specs=[qspec, kvspec, kvspec,
                  pl.BlockSpec((1, LANES), lambda b, p, i: (0, 0)),
                  pl.BlockSpec((tq, tq), lambda b, p, i: (0, 0))],
        out_specs=qspec,
        out_shape=jax.ShapeDtypeStruct((rows, w), BF16),
        scratch_shapes=[pltpu.VMEM((2, tq, LANES), F32), pltpu.VMEM((2, tq, 1), F32)],
        compiler_params=_params(3),
        name="sb_prompt",
    )(q, k, v, gain_pair, upper)


def _sb_sample_kernel(q_ref, kn_ref, vn_ref, ck_ref, cv_ref, gain_ref, up_ref, upn_ref,
                      o_ref, *, tk, n_blocks, t_new):
    qms = _head_masked(q_ref[...])
    upper = up_ref[...]
    row = lax.broadcasted_iota(jnp.int32, (t_new, t_new), 0)
    colid = lax.broadcasted_iota(jnp.int32, (t_new, t_new), 1)
    causal = colid < row
    kn = kn_ref[...]
    vn = vn_ref[...]

    accs, carries = _sb_tiles(qms, [kn], vn, [causal], upn_ref[...],
                              [jnp.zeros((t_new, 1), F32)] * 2)
    k_blocks = [ck_ref[:, kb * tk:(kb + 1) * tk].astype(BF16) for kb in range(n_blocks - 1, -1, -1)]
    res, _ = _sb_tiles(qms, k_blocks, cv_ref[...].astype(BF16), [None] * n_blocks, upper,
                       carries, feature_major=True)
    accs = [acc + r for acc, r in zip(accs, res)]
    o_ref[...] = _sb_finish(accs[0], accs[1], gain_ref[...]).astype(o_ref.dtype)


def _sb_sample(q, k_new, v_new, cache_k, cache_v, gain_pair, upper, upper_new, *, batch, tk):
    rows, w = q.shape
    t_new = rows // batch
    past = cache_k.shape[1]
    qspec = pl.BlockSpec((t_new, LANES), lambda b, p: (b, p))
    cspec = pl.BlockSpec((LANES, past), lambda b, p: (b * HEAD_PAIRS + p, 0))
    return pl.pallas_call(
        functools.partial(_sb_sample_kernel, tk=tk, n_blocks=past // tk, t_new=t_new),
        grid=(batch, HEAD_PAIRS),
        in_specs=[qspec, qspec, qspec, cspec, cspec,
                  pl.BlockSpec((1, LANES), lambda b, p: (0, 0)),
                  pl.BlockSpec((tk, tk), lambda b, p: (0, 0)),
                  pl.BlockSpec((t_new, t_new), lambda b, p: (0, 0))],
        out_specs=qspec,
        out_shape=jax.ShapeDtypeStruct((rows, w), BF16),
        compiler_params=_params(2),
        name="sb_sample",
    )(q, k_new, v_new, cache_k, cache_v, gain_pair, upper, upper_new)


def _hgrn_chains(chains, lower, ones, c):
    n = c // SUB
    pieces = []
    for _, _, g, _ in chains:
        g1 = g.astype(BF16)
        r1 = g - g1.astype(F32)
        g2 = r1.astype(BF16)
        g3 = (r1 - g2.astype(F32)).astype(BF16)
        pieces.append(jnp.concatenate([g1, g2, g3], axis=1))
    sums3 = [_dot(lower, p) for p in pieces]

    row = lax.broadcasted_iota(jnp.int32, (c, 1), 0)
    sub = lax.broadcasted_iota(jnp.int32, (1, SUB, 1), 1)
    mids = []
    for (q, k, _, _), bs in zip(chains, sums3):
        b = bs[:, :HG_DIM] + (bs[:, HG_DIM:2 * HG_DIM] + bs[:, 2 * HG_DIM:])
        b_last = b[c - 1:c, :]
        b3 = b.reshape(n, SUB, HG_DIM)
        q3 = q.reshape(n, SUB, HG_DIM)
        k3 = k.reshape(n, SUB, HG_DIM)
        kt = (k3 * jnp.exp2(b3[:, SUB - 1:SUB, :] - b3)).reshape(c, HG_DIM).astype(BF16)
        qs, ks = [], []
        zero = jnp.zeros_like(kt)
        for j in range(n - 1):
            last = SUB * (j + 1)
            qj = q[last:] * jnp.exp2(b[last:] - b[last - 1:last])
            qs.append(jnp.concatenate([jnp.zeros((last, HG_DIM), F32), qj]).astype(BF16))
            ks.append(jnp.where((row >= SUB * j) & (row < last), kt, zero))
        ps = []
        for s in range(SUB):
            e = jnp.exp2(jnp.minimum(b3 - b3[:, s:s + 1, :], 0.0))
            p = q3 * e * k3[:, s:s + 1, :]
            ps.append(jnp.where(sub >= s, p, 0.0).reshape(c, HG_DIM).astype(BF16))
        q0 = (q * jnp.exp2(b)).astype(BF16)
        khat = (k * jnp.exp2(b_last - b)).astype(BF16)
        mids.append((qs, ks, jnp.concatenate(ps, axis=0), q0, khat, jnp.exp2(b_last)))

    far = [_dot_nt(jnp.concatenate(qs, axis=1), jnp.concatenate(ks, axis=1)) if n > 1 else None
           for qs, ks, *_ in mids]
    near = [_dot(m[2], ones) for m in mids]
    colid = lax.broadcasted_iota(jnp.int32, (c, c), 1)
    base = (lax.broadcasted_iota(jnp.int32, (c, c), 0) // SUB) * SUB
    atts = []
    for a, sums in zip(far, near):
        for s in range(SUB):
            d = jnp.where(colid == base + s, sums[s * c:(s + 1) * c, :], 0.0)
            a = d if a is None else a + d
        atts.append(a.astype(BF16))
    avs = [_dot(a, ch[3]) for a, ch in zip(atts, chains)]
    kvs = [_dot_tn(ch[3], m[4]) for ch, m in zip(chains, mids)]
    return [(m[3], av, kv, m[5]) for m, av, kv in zip(mids, avs, kvs)]


def _hgrn_kernel(*refs, c, n_chunks, has_s0):
    if has_s0:
        (qh_ref, kin_ref, lf_ref, hi_ref, gt_ref, gain_ref, low_ref, ones_ref, s0_ref,
         o_ref, s_ref, st_ref) = refs
    else:
        (qh_ref, kin_ref, lf_ref, hi_ref, gt_ref, gain_ref, low_ref, ones_ref,
         o_ref, s_ref, st_ref) = refs
    step = pl.program_id(1)

    @pl.when(step == 0)
    def _():
        for h in range(HG_HEADS):
            st_ref[h] = s0_ref[0, h].T if has_s0 else jnp.zeros((HG_DIM, HG_DIM), F32)

    lower = low_ref[...]
    ones = ones_ref[...]
    gain = gain_ref[...]
    group = min(HGRN_CHUNKS_IN_FLIGHT, n_chunks)

    def chunks(ci, carry):
        where = []
        for u in range(group):
            r0 = pl.multiple_of((ci * group + u) * c, c)
            where += [(pl.ds(r0, c), slice(h * HG_DIM, (h + 1) * HG_DIM))
                      for h in range(HG_HEADS)]
        parts = _hgrn_chains(
            [(qh_ref[rows, ln], kin_ref[rows, ln], lf_ref[rows, ln], hi_ref[rows, ln])
             for rows, ln in where], lower, ones, c)
        for n, ((rows, ln), (q0, av, kv, decay)) in enumerate(zip(where, parts)):
            h = n % HG_HEADS
            st = st_ref[h]
            o = _dot_nt(q0, st.astype(BF16)) + av
            st_ref[h] = st * decay + kv
            y = _rms_rows(o, gain) * gt_ref[rows, ln]
            o_ref[rows, ln] = y.astype(o_ref.dtype)
        return carry

    lax.fori_loop(0, n_chunks // group, chunks, 0)

    @pl.when(step == pl.num_programs(1) - 1)
    def _():
        for h in range(HG_HEADS):
            s_ref[0, h] = st_ref[h].T


def _hgrn(qh, kin, lf, hi, gate, gain, s0, *, batch, c, rb):
    rows, w = qh.shape
    t = rows // batch
    nsteps = t // rb
    row_spec = pl.BlockSpec((rb, w), lambda b, s: (b * nsteps + s, 0))
    lower = jnp.tril(jnp.ones((c, c), BF16))
    ones = jnp.ones((HG_DIM, c), BF16)
    state_spec = pl.BlockSpec((1, HG_HEADS, HG_DIM, HG_DIM), lambda b, s: (b, 0, 0, 0))
    in_specs = [row_spec] * 5 + [pl.BlockSpec((1, HG_DIM), lambda b, s: (0, 0)),
                                 pl.BlockSpec((c, c), lambda b, s: (0, 0)),
                                 pl.BlockSpec((HG_DIM, c), lambda b, s: (0, 0))]
    args = [qh, kin, lf, hi, gate, gain.reshape(1, HG_DIM), lower, ones]
    if s0 is not None:
        in_specs.append(state_spec)
        args.append(s0)
    return pl.pallas_call(
        functools.partial(_hgrn_kernel, c=c, n_chunks=rb // c, has_s0=s0 is not None),
        grid=(batch, nsteps),
        in_specs=in_specs,
        out_specs=[row_spec, state_spec],
        out_shape=[jax.ShapeDtypeStruct((rows, w), BF16),
                   jax.ShapeDtypeStruct((batch, HG_HEADS, HG_DIM, HG_DIM), F32)],
        scratch_shapes=[pltpu.VMEM((HG_HEADS, HG_DIM, HG_DIM), F32)],
        compiler_params=_params(2),
        name="hgrn_s0" if s0 is not None else "hgrn",
    )(*args)


def _tile_sizes(rows, t):
    tm = min(1024, rows)
    tq = min(256, t)
    c = min(64, t)
    rb = min(512, t)
    return tm, tq, c, rb


def _column_chunks(w, fc):
    d, n = w.shape
    return w.reshape(d, n // fc, fc).transpose(1, 0, 2)


def _suffix_ones(n):
    r = lax.broadcasted_iota(jnp.int32, (n, n), 0)
    col = lax.broadcasted_iota(jnp.int32, (n, n), 1)
    return (r >= col).astype(BF16)


def _layer(x, wts, lb, *, batch, cache=None, s0=None):
    rows, _ = x.shape
    t = rows // batch
    tm, tq, c, rb = _tile_sizes(rows, t)
    fc = FFN_CHUNK
    x1 = _ffn(x, wts["ffn1_norm"], wts["ffn1_w_in"], wts["ffn1_w_out"], tm=tm, fc=fc)
    (kf, vf, qb, kb, vb, qh, kin, lf, hi, gate) = _proj(
        x1, wts["mix_norm"], wts["w_in"], wts["sb_q_gain"], wts["sb_k_gain"], lb,
        wts["head_mean"], tm=min(512, rows), batch=batch)
    gain_pair = jnp.tile(wts["sb_out_gain"], LANES // SB_HEAD_DIM).reshape(1, LANES)
    if cache is None:
        sb_o = _sb_prompt(qb, kb, vb, gain_pair, _suffix_ones(tq), batch=batch, tq=tq)
    else:
        tk = min(256, cache[0].shape[1])
        sb_o = _sb_sample(qb, kb, vb, cache[0], cache[1], gain_pair, _suffix_ones(tk),
                          _suffix_ones(t), batch=batch, tk=tk)
    hg_o, state = _hgrn(qh, kin, lf, hi, gate, wts["hg_out_gain"], s0, batch=batch, c=c, rb=rb)
    w_mix = wts["w_out"]
    x3 = _ffn(x1, wts["ffn2_norm"], wts["ffn2_w_in"], wts["ffn2_w_out"],
              mix=(sb_o, hg_o, w_mix[:SB_WIDTH], w_mix[SB_WIDTH:]), tm=tm, fc=fc)
    return x3, kf, vf, state


def _heads(a, batch, t):
    if a.ndim == 3:
        return a.reshape(batch, SB_HEADS, SB_HEAD_DIM, t).transpose(0, 3, 1, 2)
    return a.reshape(batch, t, SB_HEADS, SB_HEAD_DIM)


def kernel(x_prompt, x_sample, cache_sb_k, cache_sb_v, state_hgrn, ffn1_norm, ffn1_w_in,
           ffn1_w_out, mix_norm, w_in, sb_q_gain, sb_k_gain, hg_lb_logits, sb_out_gain,
           hg_out_gain, w_out, ffn2_norm, ffn2_w_in, ffn2_w_out):
    depth = w_in.shape[0]
    bp, tp, d = x_prompt.shape
    bs, ts, _ = x_sample.shape
    past = cache_sb_k.shape[2]
    lb_all = jnp.cumsum(jax.nn.softmax(hg_lb_logits.astype(F32), axis=0), axis=0)
    hid = lax.broadcasted_iota(jnp.int32, (SB_WIDTH, SB_WIDTH), 0) // SB_HEAD_DIM
    hid_t = lax.broadcasted_iota(jnp.int32, (SB_WIDTH, SB_WIDTH), 1) // SB_HEAD_DIM
    head_mean = jnp.where(hid == hid_t, 1.0 / SB_HEAD_DIM, 0.0).astype(BF16)

    xp = x_prompt.reshape(bp * tp, d)
    xs = x_sample.reshape(bs * ts, d)
    outs = [[] for _ in range(6)]
    for l in range(depth):
        wts = dict(
            ffn1_norm=ffn1_norm[l],
            ffn1_w_in=_column_chunks(ffn1_w_in[l].astype(BF16), FFN_CHUNK),
            ffn1_w_out=ffn1_w_out[l].astype(BF16), mix_norm=mix_norm[l],
            w_in=w_in[l].astype(BF16), sb_q_gain=sb_q_gain[l], sb_k_gain=sb_k_gain[l],
            sb_out_gain=sb_out_gain[l], hg_out_gain=hg_out_gain[l],
            w_out=w_out[l].astype(BF16), ffn2_norm=ffn2_norm[l],
            ffn2_w_in=_column_chunks(ffn2_w_in[l].astype(BF16), FFN_CHUNK),
            ffn2_w_out=ffn2_w_out[l].astype(BF16),
            head_mean=head_mean)
        xp, kp, vp, sp = _layer(xp, wts, lb_all[l], batch=bp)
        cache = tuple(c[l].transpose(0, 2, 3, 1).reshape(bs * SB_WIDTH, past)
                      for c in (cache_sb_k, cache_sb_v))
        xs, kn, vn, sn = _layer(xs, wts, lb_all[l], batch=bs, cache=cache, s0=state_hgrn[l])
        for lst, val in zip(outs, (_heads(kp, bp, tp), _heads(vp, bp, tp), sp,
                                   _heads(kn, bs, ts), _heads(vn, bs, ts), sn)):
            lst.append(val)
    return (xp.reshape(bp, tp, d), xs.reshape(bs, ts, d)) + tuple(
        jnp.stack(lst, axis=0) for lst in outs)
```

```python
import functools

import jax
import jax.numpy as jnp
from jax import lax
from jax.experimental import pallas as pl
from jax.experimental.pallas import tpu as pltpu

EPS = 1e-6
SB_HEADS = 8
SB_HEAD_DIM = 64
SB_WIDTH = SB_HEADS * SB_HEAD_DIM
SB_SCALE = SB_HEAD_DIM ** -0.5
LOG2_E = 1.4426950408889634
EXP2_CLAMP = 126.0
HG_HEADS = 4
HG_DIM = 128
HG_WIDTH = HG_HEADS * HG_DIM
LANES = 128
SUB = 8
HEAD_PAIRS = SB_WIDTH // LANES

BF16 = jnp.bfloat16
F32 = jnp.float32

VMEM_LIMIT = 56 * 1024 * 1024
FFN_CHUNK = 256
SB_BLOCKS_IN_FLIGHT = 16
HGRN_CHUNKS_IN_FLIGHT = 4


def _dot(a, b):
    return jnp.dot(a, b, preferred_element_type=F32)


def _dot_nt(a, b):
    return lax.dot_general(a, b, (((1,), (1,)), ((), ())), preferred_element_type=F32)


def _dot_tn(a, b):
    return lax.dot_general(a, b, (((0,), (0,)), ((), ())), preferred_element_type=F32)


def _sigmoid(x):
    return 1.0 / (1.0 + jnp.exp(-x))


def _rms_rows(x, g):
    ms = jnp.mean(x * x, axis=-1, keepdims=True)
    return x * lax.rsqrt(ms + EPS) * g


def _params(n_grid):
    return pltpu.CompilerParams(
        dimension_semantics=("arbitrary",) * n_grid, vmem_limit_bytes=VMEM_LIMIT)


def _ffn_kernel(*refs, with_mix):
    if with_mix:
        (x_ref, ma_ref, mb_ref, woa_ref, wob_ref, g_ref, wa_ref, wb_ref, wo_ref,
         o_ref, h_ref) = refs
    else:
        x_ref, g_ref, wa_ref, wb_ref, wo_ref, o_ref, h_ref = refs

    @pl.when(pl.program_id(1) == 0)
    def _():
        x = x_ref[...]
        if with_mix:
            x = x + _dot(ma_ref[...], woa_ref[...]) + _dot(mb_ref[...], wob_ref[...])
        o_ref[...] = x
        h_ref[...] = _rms_rows(x, g_ref[...]).astype(BF16)

    h = h_ref[...]
    a = _dot(h, wa_ref[0])
    b = _dot(h, wb_ref[0])
    act = a * (0.5 * _sigmoid(a)) * b
    o_ref[...] += _dot(act.astype(BF16), wo_ref[...])


def _ffn(x, norm_g, w_in, w_out, mix=None, *, tm, fc):
    rows, d = x.shape
    nj = w_out.shape[0] // fc
    row_spec = pl.BlockSpec((tm, d), lambda i, j: (i, 0))
    in_specs = [row_spec]
    args = [x]
    if mix is not None:
        ma, mb, woa, wob = mix
        in_specs += [pl.BlockSpec((tm, ma.shape[1]), lambda i, j: (i, 0)),
                     pl.BlockSpec((tm, mb.shape[1]), lambda i, j: (i, 0)),
                     pl.BlockSpec(woa.shape, lambda i, j: (0, 0)),
                     pl.BlockSpec(wob.shape, lambda i, j: (0, 0))]
        args += [ma, mb, woa, wob]
    in_specs += [pl.BlockSpec((1, d), lambda i, j: (0, 0)),
                 pl.BlockSpec((1, d, fc), lambda i, j: (j, 0, 0)),
                 pl.BlockSpec((1, d, fc), lambda i, j: (j + nj, 0, 0)),
                 pl.BlockSpec((fc, d), lambda i, j: (j, 0))]
    args += [norm_g.reshape(1, d), w_in, w_in, w_out]
    return pl.pallas_call(
        functools.partial(_ffn_kernel, with_mix=mix is not None),
        grid=(rows // tm, nj),
        in_specs=in_specs,
        out_specs=row_spec,
        out_shape=jax.ShapeDtypeStruct((rows, d), F32),
        scratch_shapes=[pltpu.VMEM((tm, d), BF16)],
        compiler_params=_params(2),
        name="ffn_mix" if mix is not None else "ffn",
    )(*args)


def _proj_kernel(x_ref, g_ref, w_ref, qg_ref, kg_ref, lb_ref, hm_ref,
                 kf_ref, vf_ref, qb_ref, kb_ref, vb_ref,
                 qh_ref, kin_ref, lf_ref, hi_ref, gt_ref, *, feature_major):
    h = _rms_rows(x_ref[...], g_ref[...]).astype(BF16)
    w = SB_WIDTH

    def col(c):
        return _dot(h, w_ref[:, c * w:(c + 1) * w])

    def head_norm(a, gain):
        ms = _dot((a * a).astype(BF16), hm_ref[...])
        return a * lax.rsqrt(ms + EPS) * gain

    qn = head_norm(col(0), qg_ref[...])
    qb_ref[...] = (qn * (SB_SCALE * LOG2_E)).astype(BF16)
    kn = head_norm(col(1), kg_ref[...])
    sv = col(2)
    if feature_major:
        kf_ref[0] = kn.T
        vf_ref[0] = sv.T
    else:
        kf_ref[...] = kn
        vf_ref[...] = sv
    kb_ref[...] = kn.astype(BF16)
    vb_ref[...] = sv.astype(BF16)

    hq = col(3)
    qh_ref[...] = hq * _sigmoid(hq)
    hf = col(4)
    lb = lb_ref[...]
    f = lb + (1.0 - lb) * _sigmoid(hf)
    lf_ref[...] = jnp.log2(f)
    kin_ref[...] = (1.0 - lb) * _sigmoid(-hf)
    hi_ref[...] = col(5).astype(BF16)
    hg = col(6)
    gt_ref[...] = hg * _sigmoid(hg)


def _proj(x, norm_g, w_in, q_gain, k_gain, lb, head_mean, *, tm, batch):
    rows, d = x.shape
    t = rows // batch
    w = SB_WIDTH
    feature_major = t % tm == 0
    row_spec = pl.BlockSpec((tm, w), lambda i: (i, 0))
    vec = lambda n: pl.BlockSpec((1, n), lambda i: (0, 0))
    f32o = jax.ShapeDtypeStruct((rows, w), F32)
    bf16o = jax.ShapeDtypeStruct((rows, w), BF16)
    if feature_major:
        nt = t // tm
        kv_spec = pl.BlockSpec((1, w, tm), lambda i: (i // nt, 0, i % nt))
        kv_shape = jax.ShapeDtypeStruct((batch, w, t), F32)
    else:
        kv_spec, kv_shape = row_spec, f32o
    return pl.pallas_call(
        functools.partial(_proj_kernel, feature_major=feature_major),
        grid=(rows // tm,),
        in_specs=[pl.BlockSpec((tm, d), lambda i: (i, 0)), vec(d),
                  pl.BlockSpec(w_in.shape, lambda i: (0, 0)),
                  vec(w), vec(w), vec(w),
                  pl.BlockSpec(head_mean.shape, lambda i: (0, 0))],
        out_specs=[kv_spec, kv_spec] + [row_spec] * 8,
        out_shape=[kv_shape, kv_shape, bf16o, bf16o, bf16o, f32o, f32o, f32o, bf16o, f32o],
        compiler_params=_params(1),
        name="proj",
    )(x, norm_g.reshape(1, d), w_in,
      jnp.tile(q_gain, SB_HEADS).reshape(1, w), jnp.tile(k_gain, SB_HEADS).reshape(1, w),
      lb.reshape(1, w), head_mean)


def _sb_tiles(qs, k_blocks, v_slab, masks, upper, carries, feature_major=False):
    scores, weighted = (_dot, _dot_nt) if feature_major else (_dot_nt, _dot)
    tiles = [(h, n) for n in range(len(k_blocks)) for h in range(len(qs))]
    zs = [scores(qs[h], k_blocks[n]) for h, n in tiles]
    suffixes = []
    for z, (_, n) in zip(zs, tiles):
        drop = jnp.maximum(z, jnp.log2(1.0 + jnp.exp2(jnp.minimum(z, EXP2_CLAMP))))
        if masks[n] is not None:
            drop = jnp.where(masks[n], drop, 0.0)
        suffixes.append(_dot(drop.astype(BF16), upper))
    carries = list(carries)
    weights = [[] for _ in qs]
    for z, suffix, (h, n) in zip(zs, suffixes, tiles):
        w = jnp.exp2(z - suffix - carries[h])
        if masks[n] is not None:
            w = jnp.where(masks[n], w, 0.0)
        weights[h].insert(0, w.astype(BF16))
        carries[h] = carries[h] + suffix[:, :1]
    return [weighted(jnp.concatenate(ws, axis=1), v_slab) for ws in weights], carries


def _sb_finish(acc0, acc1, gain):
    lane = lax.broadcasted_iota(jnp.int32, (1, LANES), 1)
    first = lane < SB_HEAD_DIM
    o = jnp.where(first, acc0, acc1)
    sq = o * o
    s0 = jnp.sum(jnp.where(first, sq, 0.0), axis=-1, keepdims=True)
    s1 = jnp.sum(jnp.where(first, 0.0, sq), axis=-1, keepdims=True)
    ms = jnp.where(first, s0, s1) * (1.0 / SB_HEAD_DIM)
    return o * lax.rsqrt(ms + EPS) * gain


def _head_masked(q):
    lane = lax.broadcasted_iota(jnp.int32, (1, LANES), 1)
    first = lane < SB_HEAD_DIM
    zero = jnp.zeros_like(q)
    return jnp.where(first, q, zero), jnp.where(first, zero, q)


def _sb_prompt_kernel(q_ref, k_ref, v_ref, gain_ref, up_ref, o_ref, acc_ref, car_ref, *, tq):
    i = pl.program_id(2)
    qms = _head_masked(q_ref[...])
    upper = up_ref[...]
    row = lax.broadcasted_iota(jnp.int32, (tq, tq), 0)
    colid = lax.broadcasted_iota(jnp.int32, (tq, tq), 1)
    causal = colid < row

    def blocks(top, count, diagonal_first):
        k_blocks = [k_ref[pl.ds(pl.multiple_of((top - n) * tq, tq), tq), :] for n in range(count)]
        oldest = pl.multiple_of((top - count + 1) * tq, tq)
        masks = [causal if diagonal_first and n == 0 else None for n in range(count)]
        res, carries = _sb_tiles(qms, k_blocks, v_ref[pl.ds(oldest, count * tq), :], masks,
                                 upper, [car_ref[0], car_ref[1]])
        for h in range(2):
            car_ref[h] = carries[h]
            acc_ref[h] += res[h]

    acc_ref[...] = jnp.zeros_like(acc_ref)
    car_ref[...] = jnp.zeros_like(car_ref)
    for rem in range(SB_BLOCKS_IN_FLIGHT):
        @pl.when(i % SB_BLOCKS_IN_FLIGHT == rem)
        def _(rem=rem):
            blocks(i, rem + 1, True)

    n_groups = i // SB_BLOCKS_IN_FLIGHT

    def group(j, carry):
        top = SB_BLOCKS_IN_FLIGHT * (n_groups - j) - 1
        blocks(top, SB_BLOCKS_IN_FLIGHT, False)
        return carry

    lax.fori_loop(0, n_groups, group, 0)
    o_ref[...] = _sb_finish(acc_ref[0], acc_ref[1], gain_ref[...]).astype(o_ref.dtype)


def _sb_prompt(q, k, v, gain_pair, upper, *, batch, tq):
    rows, w = q.shape
    t = rows // batch
    nq = t // tq
    qspec = pl.BlockSpec((tq, LANES), lambda b, p, i: (b * nq + i, p))
    kvspec = pl.BlockSpec((t, LANES), lambda b, p, i: (b, p))
    return pl.pallas_call(
        functools.partial(_sb_prompt_kernel, tq=tq),
        grid=(batch, HEAD_PAIRS, nq),
        in_specs=[qspec, kvspec, kvspec,
                  pl.BlockSpec((1, LANES), lambda b, p, i: (0, 0)),
                  pl.BlockSpec((tq, tq), lambda b, p, i: (0, 0))],
        out_specs=qspec,
        out_shape=jax.ShapeDtypeStruct((rows, w), BF16),
        scratch_shapes=[pltpu.VMEM((2, tq, LANES), F32), pltpu.VMEM((2, tq, 1), F32)],
        compiler_params=_params(3),
        name="sb_prompt",
    )(q, k, v, gain_pair, upper)


def _sb_sample_kernel(q_ref, kn_ref, vn_ref, ck_ref, cv_ref, gain_ref, up_ref, upn_ref,
                      o_ref, *, tk, n_blocks, t_new):
    qms = _head_masked(q_ref[...])
    upper = up_ref[...]
    row = lax.broadcasted_iota(jnp.int32, (t_new, t_new), 0)
    colid = lax.broadcasted_iota(jnp.int32, (t_new, t_new), 1)
    causal = colid < row
    kn = kn_ref[...]
    vn = vn_ref[...]

    accs, carries = _sb_tiles(qms, [kn], vn, [causal], upn_ref[...],
                              [jnp.zeros((t_new, 1), F32)] * 2)
    k_blocks = [ck_ref[:, kb * tk:(kb + 1) * tk].astype(BF16) for kb in range(n_blocks - 1, -1, -1)]
    res, _ = _sb_tiles(qms, k_blocks, cv_ref[...].astype(BF16), [None] * n_blocks, upper,
                       carries, feature_major=True)
    accs = [acc + r for acc, r in zip(accs, res)]
    o_ref[...] = _sb_finish(accs[0], accs[1], gain_ref[...]).astype(o_ref.dtype)


def _sb_sample(q, k_new, v_new, cache_k, cache_v, gain_pair, upper, upper_new, *, batch, tk):
    rows, w = q.shape
    t_new = rows // batch
    past = cache_k.shape[1]
    qspec = pl.BlockSpec((t_new, LANES), lambda b, p: (b, p))
    cspec = pl.BlockSpec((LANES, past), lambda b, p: (b * HEAD_PAIRS + p, 0))
    return pl.pallas_call(
        functools.partial(_sb_sample_kernel, tk=tk, n_blocks=past // tk, t_new=t_new),
        grid=(batch, HEAD_PAIRS),
        in_specs=[qspec, qspec, qspec, cspec, cspec,
                  pl.BlockSpec((1, LANES), lambda b, p: (0, 0)),
                  pl.BlockSpec((tk, tk), lambda b, p: (0, 0)),
                  pl.BlockSpec((t_new, t_new), lambda b, p: (0, 0))],
        out_specs=qspec,
        out_shape=jax.ShapeDtypeStruct((rows, w), BF16),
        compiler_params=_params(2),
        name="sb_sample",
    )(q, k_new, v_new, cache_k, cache_v, gain_pair, upper, upper_new)


def _hgrn_chains(chains, lower, ones, c):
    n = c // SUB
    pieces = []
    for _, _, g, _ in chains:
        g1 = g.astype(BF16)
        r1 = g - g1.astype(F32)
        g2 = r1.astype(BF16)
        g3 = (r1 - g2.astype(F32)).astype(BF16)
        pieces.append(jnp.concatenate([g1, g2, g3], axis=1))
    sums3 = [_dot(lower, p) for p in pieces]

    row = lax.broadcasted_iota(jnp.int32, (c, 1), 0)
    sub = lax.broadcasted_iota(jnp.int32, (1, SUB, 1), 1)
    mids = []
    for (q, k, _, _), bs in zip(chains, sums3):
        b = bs[:, :HG_DIM] + (bs[:, HG_DIM:2 * HG_DIM] + bs[:, 2 * HG_DIM:])
        b_last = b[c - 1:c, :]
        b3 = b.reshape(n, SUB, HG_DIM)
        q3 = q.reshape(n, SUB, HG_DIM)
        k3 = k.reshape(n, SUB, HG_DIM)
        kt = (k3 * jnp.exp2(b3[:, SUB - 1:SUB, :] - b3)).reshape(c, HG_DIM).astype(BF16)
        qs, ks = [], []
        zero = jnp.zeros_like(kt)
        for j in range(n - 1):
            last = SUB * (j + 1)
            qj = q[last:] * jnp.exp2(b[last:] - b[last - 1:last])
            qs.append(jnp.concatenate([jnp.zeros((last, HG_DIM), F32), qj]).astype(BF16))
            ks.append(jnp.where((row >= SUB * j) & (row < last), kt, zero))
        ps = []
        for s in range(SUB):
            e = jnp.exp2(jnp.minimum(b3 - b3[:, s:s + 1, :], 0.0))
            p = q3 * e * k3[:, s:s + 1, :]
            ps.append(jnp.where(sub >= s, p, 0.0).reshape(c, HG_DIM).astype(BF16))
        q0 = (q * jnp.exp2(b)).astype(BF16)
        khat = (k * jnp.exp2(b_last - b)).astype(BF16)
        mids.append((qs, ks, jnp.concatenate(ps, axis=0), q0, khat, jnp.exp2(b_last)))

    far = [_dot_nt(jnp.concatenate(qs, axis=1), jnp.concatenate(ks, axis=1)) if n > 1 else None
           for qs, ks, *_ in mids]
    near = [_dot(m[2], ones) for m in mids]
    colid = lax.broadcasted_iota(jnp.int32, (c, c), 1)
    base = (lax.broadcasted_iota(jnp.int32, (c, c), 0) // SUB) * SUB
    atts = []
    for a, sums in zip(far, near):
        for s in range(SUB):
            d = jnp.where(colid == base + s, sums[s * c:(s + 1) * c, :], 0.0)
            a = d if a is None else a + d
        atts.append(a.astype(BF16))
    avs = [_dot(a, ch[3]) for a, ch in zip(atts, chains)]
    kvs = [_dot_tn(ch[3], m[4]) for ch, m in zip(chains, mids)]
    return [(m[3], av, kv, m[5]) for m, av, kv in zip(mids, avs, kvs)]


def _hgrn_kernel(*refs, c, n_chunks, has_s0):
    if has_s0:
        (qh_ref, kin_ref, lf_ref, hi_ref, gt_ref, gain_ref, low_ref, ones_ref, s0_ref,
         o_ref, s_ref, st_ref) = refs
    else:
        (qh_ref, kin_ref, lf_ref, hi_ref, gt_ref, gain_ref, low_ref, ones_ref,
         o_ref, s_ref, st_ref) = refs
    step = pl.program_id(1)

    @pl.when(step == 0)
    def _():
        for h in range(HG_HEADS):
            st_ref[h] = s0_ref[0, h].T if has_s0 else jnp.zeros((HG_DIM, HG_DIM), F32)

    lower = low_ref[...]
    ones = ones_ref[...]
    gain = gain_ref[...]
    group = min(HGRN_CHUNKS_IN_FLIGHT, n_chunks)

    def chunks(ci, carry):
        where = []
        for u in range(group):
            r0 = pl.multiple_of((ci * group + u) * c, c)
            where += [(pl.ds(r0, c), slice(h * HG_DIM, (h + 1) * HG_DIM))
                      for h in range(HG_HEADS)]
        parts = _hgrn_chains(
            [(qh_ref[rows, ln], kin_ref[rows, ln], lf_ref[rows, ln], hi_ref[rows, ln])
             for rows, ln in where], lower, ones, c)
        for n, ((rows, ln), (q0, av, kv, decay)) in enumerate(zip(where, parts)):
            h = n % HG_HEADS
            st = st_ref[h]
            o = _dot_nt(q0, st.astype(BF16)) + av
            st_ref[h] = st * decay + kv
            y = _rms_rows(o, gain) * gt_ref[rows, ln]
            o_ref[rows, ln] = y.astype(o_ref.dtype)
        return carry

    lax.fori_loop(0, n_chunks // group, chunks, 0)

    @pl.when(step == pl.num_programs(1) - 1)
    def _():
        for h in range(HG_HEADS):
            s_ref[0, h] = st_ref[h].T


def _hgrn(qh, kin, lf, hi, gate, gain, s0, *, batch, c, rb):
    rows, w = qh.shape
    t = rows // batch
    nsteps = t // rb
    row_spec = pl.BlockSpec((rb, w), lambda b, s: (b * nsteps + s, 0))
    lower = jnp.tril(jnp.ones((c, c), BF16))
    ones = jnp.ones((HG_DIM, c), BF16)
    state_spec = pl.BlockSpec((1, HG_HEADS, HG_DIM, HG_DIM), lambda b, s: (b, 0, 0, 0))
    in_specs = [row_spec] * 5 + [pl.BlockSpec((1, HG_DIM), lambda b, s: (0, 0)),
                                 pl.BlockSpec((c, c), lambda b, s: (0, 0)),
                                 pl.BlockSpec((HG_DIM, c), lambda b, s: (0, 0))]
    args = [qh, kin, lf, hi, gate, gain.reshape(1, HG_DIM), lower, ones]
    if s0 is not None:
        in_specs.append(state_spec)
        args.append(s0)
    return pl.pallas_call(
        functools.partial(_hgrn_kernel, c=c, n_chunks=rb // c, has_s0=s0 is not None),
        grid=(batch, nsteps),
        in_specs=in_specs,
        out_specs=[row_spec, state_spec],
        out_shape=[jax.ShapeDtypeStruct((rows, w), BF16),
                   jax.ShapeDtypeStruct((batch, HG_HEADS, HG_DIM, HG_DIM), F32)],
        scratch_shapes=[pltpu.VMEM((HG_HEADS, HG_DIM, HG_DIM), F32)],
        compiler_params=_params(2),
        name="hgrn_s0" if s0 is not None else "hgrn",
    )(*args)


def _tile_sizes(rows, t):
    tm = min(1024, rows)
    tq = min(256, t)
    c = min(64, t)
    rb = min(512, t)
    return tm, tq, c, rb


def _column_chunks(w, fc):
    d, n = w.shape
    return w.reshape(d, n // fc, fc).transpose(1, 0, 2)


def _suffix_ones(n):
    r = lax.broadcasted_iota(jnp.int32, (n, n), 0)
    col = lax.broadcasted_iota(jnp.int32, (n, n), 1)
    return (r >= col).astype(BF16)


def _layer(x, wts, lb, *, batch, cache=None, s0=None):
    rows, _ = x.shape
    t = rows // batch
    tm, tq, c, rb = _tile_sizes(rows, t)
    fc = FFN_CHUNK
    x1 = _ffn(x, wts["ffn1_norm"], wts["ffn1_w_in"], wts["ffn1_w_out"], tm=tm, fc=fc)
    (kf, vf, qb, kb, vb, qh, kin, lf, hi, gate) = _proj(
        x1, wts["mix_norm"], wts["w_in"], wts["sb_q_gain"], wts["sb_k_gain"], lb,
        wts["head_mean"], tm=min(512, rows), batch=batch)
    gain_pair = jnp.tile(wts["sb_out_gain"], LANES // SB_HEAD_DIM).reshape(1, LANES)
    if cache is None:
        sb_o = _sb_prompt(qb, kb, vb, gain_pair, _suffix_ones(tq), batch=batch, tq=tq)
    else:
        tk = min(256, cache[0].shape[1])
        sb_o = _sb_sample(qb, kb, vb, cache[0], cache[1], gain_pair, _suffix_ones(tk),
                          _suffix_ones(t), batch=batch, tk=tk)
    hg_o, state = _hgrn(qh, kin, lf, hi, gate, wts["hg_out_gain"], s0, batch=batch, c=c, rb=rb)
    w_mix = wts["w_out"]
    x3 = _ffn(x1, wts["ffn2_norm"], wts["ffn2_w_in"], wts["ffn2_w_out"],
              mix=(sb_o, hg_o, w_mix[:SB_WIDTH], w_mix[SB_WIDTH:]), tm=tm, fc=fc)
    return x3, kf, vf, state


def _heads(a, batch, t):
    if a.ndim == 3:
        return a.reshape(batch, SB_HEADS, SB_HEAD_DIM, t).transpose(0, 3, 1, 2)
    return a.reshape(batch, t, SB_HEADS, SB_HEAD_DIM)


def kernel(x_prompt, x_sample, cache_sb_k, cache_sb_v, state_hgrn, ffn1_norm, ffn1_w_in,
           ffn1_w_out, mix_norm, w_in, sb_q_gain, sb_k_gain, hg_lb_logits, sb_out_gain,
           hg_out_gain, w_out, ffn2_norm, ffn2_w_in, ffn2_w_out):
    depth = w_in.shape[0]
    bp, tp, d = x_prompt.shape
    bs, ts, _ = x_sample.shape
    past = cache_sb_k.shape[2]
    lb_all = jnp.cumsum(jax.nn.softmax(hg_lb_logits.astype(F32), axis=0), axis=0)
    hid = lax.broadcasted_iota(jnp.int32, (SB_WIDTH, SB_WIDTH), 0) // SB_HEAD_DIM
    hid_t = lax.broadcasted_iota(jnp.int32, (SB_WIDTH, SB_WIDTH), 1) // SB_HEAD_DIM
    head_mean = jnp.where(hid == hid_t, 1.0 / SB_HEAD_DIM, 0.0).astype(BF16)

    xp = x_prompt.reshape(bp * tp, d)
    xs = x_sample.reshape(bs * ts, d)
    outs = [[] for _ in range(6)]
    for l in range(depth):
        wts = dict(
            ffn1_norm=ffn1_norm[l],
            ffn1_w_in=_column_chunks(ffn1_w_in[l].astype(BF16), FFN_CHUNK),
            ffn1_w_out=ffn1_w_out[l].astype(BF16), mix_norm=mix_norm[l],
            w_in=w_in[l].astype(BF16), sb_q_gain=sb_q_gain[l], sb_k_gain=sb_k_gain[l],
            sb_out_gain=sb_out_gain[l], hg_out_gain=hg_out_gain[l],
            w_out=w_out[l].astype(BF16), ffn2_norm=ffn2_norm[l],
            ffn2_w_in=_column_chunks(ffn2_w_in[l].astype(BF16), FFN_CHUNK),
            ffn2_w_out=ffn2_w_out[l].astype(BF16),
            head_mean=head_mean)
        xp, kp, vp, sp = _layer(xp, wts, lb_all[l], batch=bp)
        cache = tuple(c[l].transpose(0, 2, 3, 1).reshape(bs * SB_WIDTH, past)
                      for c in (cache_sb_k, cache_sb_v))
        xs, kn, vn, sn = _layer(xs, wts, lb_all[l], batch=bs, cache=cache, s0=state_hgrn[l])
        for lst, val in zip(outs, (_heads(kp, bp, tp), _heads(vp, bp, tp), sp,
                                   _heads(kn, bs, ts), _heads(vn, bs, ts), sn)):
            lst.append(val)
    return (xp.reshape(bp, tp, d), xs.reshape(bs, ts, d)) + tuple(
        jnp.stack(lst, axis=0) for lst in outs)
```
